```python
import jax, jax.numpy as jnp
from jax import lax
import numpy as np

D_MODEL = 4096
BATCH = 2
SEQ = 8192
DEPTH = 2

N_A_LAYERS = DEPTH // 2
N_B_LAYERS = DEPTH - N_A_LAYERS
MIX_WIDTH = D_MODEL
MEM_LEN = 256
MEM_HEADS = 4
MEM_WIDTH = MIX_WIDTH // 4
MEM_HEAD_DIM = MEM_WIDTH // MEM_HEADS
CONV_CH = MIX_WIDTH - MEM_WIDTH
CONV_WIDTH = 31
MLA_V_DIM = 128
MLA_HEADS = (MIX_WIDTH - MEM_WIDTH) // MLA_V_DIM
QK_NOPE = 128
QK_ROPE = 64
QK_DIM = QK_NOPE + QK_ROPE
Q_LORA = D_MODEL // 4
KV_LORA = D_MODEL // 8
ROPE_THETA = 10000.0
D_FF = 4 * D_MODEL
BLOCK_Q = 128
EPS = 1e-6

kernel_name = 'yoco_conformer_mla_hybrid'


def rms_norm(x, g):
    xf = x.astype(jnp.float32)
    y = xf * lax.rsqrt(jnp.mean(xf * xf, axis=-1, keepdims=True) + EPS)
    return (y * g.astype(jnp.float32)).astype(x.dtype)


def layer_norm(x, g, b):
    xf = x.astype(jnp.float32)
    mu = jnp.mean(xf, axis=-1, keepdims=True)
    var = jnp.mean(jnp.square(xf - mu), axis=-1, keepdims=True)
    y = (xf - mu) * lax.rsqrt(var + EPS)
    return (y * g.astype(jnp.float32) + b.astype(jnp.float32)).astype(x.dtype)


def rope_tables(positions):
    inv_freq = ROPE_THETA ** (-jnp.arange(0, QK_ROPE, 2, dtype=jnp.float32) / QK_ROPE)
    ang = positions.astype(jnp.float32)[..., None] * inv_freq
    return jnp.cos(ang), jnp.sin(ang)


def apply_rope(t, cos, sin):
    cos = cos.astype(t.dtype)
    sin = sin.astype(t.dtype)
    t1, t2 = jnp.split(t, 2, axis=-1)
    return jnp.concatenate([t1 * cos - t2 * sin, t2 * cos + t1 * sin], axis=-1)


def memory_attention(q, mem, g_norm, w_kv, g_q, g_k):
    b, s, _ = q.shape
    m = mem.shape[1]
    q = rms_norm(q.reshape(b, s, MEM_HEADS, MEM_HEAD_DIM), g_q)
    kv = rms_norm(mem, g_norm) @ w_kv
    k, v = jnp.split(kv, 2, axis=-1)
    k = rms_norm(k.reshape(b, m, MEM_HEADS, MEM_HEAD_DIM), g_k)
    v = v.reshape(b, m, MEM_HEADS, MEM_HEAD_DIM)
    sc = jnp.einsum('bshd,bmhd->bhsm', q, k).astype(jnp.float32) * (MEM_HEAD_DIM ** -0.5)
    p = jax.nn.softmax(sc, axis=-1).astype(v.dtype)
    return jnp.einsum('bhsm,bmhd->bshd', p, v).reshape(b, s, MEM_WIDTH)


def conformer_conv(u, conv_w, conv_b, ln_g, ln_b):
    a, gate = jnp.split(u, 2, axis=-1)
    g = a * jax.nn.sigmoid(gate)
    c = lax.conv_general_dilated(
        g, conv_w[:, None, :].astype(g.dtype), window_strides=(1,),
        padding=[(CONV_WIDTH - 1, 0)],
        dimension_numbers=('NWC', 'WIO', 'NWC'),
        feature_group_count=CONV_CH) + conv_b
    return jax.nn.silu(layer_norm(c, ln_g, ln_b))


def mla_attention(q_nope, q_rope, k_nope, k_rope, v):
    b, s, h, _ = q_nope.shape
    nb = s // BLOCK_Q
    key_idx = jnp.arange(s)
    scale = QK_DIM ** -0.5

    def to_blocks(t):
        return jnp.moveaxis(t.reshape(b, nb, BLOCK_Q, *t.shape[2:]), 1, 0)

    def one_block(args):
        qn, qr, start = args
        sc = (jnp.einsum('bqhd,bkhd->bhqk', qn, k_nope)
              + jnp.einsum('bqhd,bkd->bhqk', qr, k_rope)).astype(jnp.float32) * scale
        q_idx = start + jnp.arange(BLOCK_Q)
        sc = jnp.where(key_idx[None, :] <= q_idx[:, None], sc, -jnp.inf)
        p = jax.nn.softmax(sc, axis=-1).astype(v.dtype)
        return jnp.einsum('bhqk,bkhd->bqhd', p, v)

    starts = jnp.arange(nb) * BLOCK_Q
    o = lax.map(one_block, (to_blocks(q_nope), to_blocks(q_rope), starts))
    return jnp.moveaxis(o, 0, 1).reshape(b, s, h * v.shape[-1])


def setup_inputs(seed: int = 0) -> dict:
    key = jax.random.key(seed)
    ks = iter(jax.random.split(key, 48))

    def nrm(shape, scale):
        return jax.random.normal(next(ks), shape, jnp.float32) * scale

    def gain(shape):
        return 1.0 + 0.02 * jax.random.normal(next(ks), shape, jnp.float32)

    x = nrm((BATCH, SEQ, D_MODEL), 1.0)
    mem = nrm((BATCH, MEM_LEN, D_MODEL), 1.0)
    offs = jax.random.randint(next(ks), (BATCH, 1), 0, 1024, jnp.int32)
    positions = jnp.arange(SEQ, dtype=jnp.int32)[None, :] + offs
    return {
        'x': x, 'mem': mem, 'positions': positions,
        'norm_mix': gain((DEPTH, D_MODEL)),
        'norm_mlp': gain((DEPTH, D_MODEL)),
        'norm_mem': gain((DEPTH, D_MODEL)),
        'w_mem_kv': nrm((DEPTH, D_MODEL, 2 * MEM_WIDTH), D_MODEL ** -0.5),
        'g_mem_q': gain((DEPTH, MEM_HEAD_DIM)),
        'g_mem_k': gain((DEPTH, MEM_HEAD_DIM)),
        'w_mlp_in': nrm((DEPTH, D_MODEL, D_FF), D_MODEL ** -0.5),
        'w_mlp_out': nrm((DEPTH, D_FF, D_MODEL), D_FF ** -0.5),
        'a_w_in': nrm((N_A_LAYERS, D_MODEL, 2 * CONV_CH + MEM_WIDTH), D_MODEL ** -0.5),
        'a_conv_w': nrm((N_A_LAYERS, CONV_WIDTH, CONV_CH), CONV_WIDTH ** -0.5),
        'a_conv_b': nrm((N_A_LAYERS, CONV_CH), 0.02),
        'a_ln_g': gain((N_A_LAYERS, CONV_CH)),
        'a_ln_b': nrm((N_A_LAYERS, CONV_CH), 0.02),
        'a_w_out': nrm((N_A_LAYERS, CONV_CH + MEM_WIDTH, D_MODEL), MIX_WIDTH ** -0.5),
        'b_w_in': nrm((N_B_LAYERS, D_MODEL, Q_LORA + MEM_WIDTH), D_MODEL ** -0.5),
        'b_g_qa': gain((N_B_LAYERS, Q_LORA)),
        'b_w_uq': nrm((N_B_LAYERS, Q_LORA, MLA_HEADS * QK_DIM), Q_LORA ** -0.5),
        'b_g_qn': gain((N_B_LAYERS, QK_NOPE)),
        'b_g_qr': gain((N_B_LAYERS, QK_ROPE)),
        'b_w_out': nrm((N_B_LAYERS, MLA_HEADS * MLA_V_DIM + MEM_WIDTH, D_MODEL), MIX_WIDTH ** -0.5),
        'kv_g_in': gain((D_MODEL,)),
        'kv_w_dkv': nrm((D_MODEL, KV_LORA + QK_ROPE), D_MODEL ** -0.5),
        'kv_g_a': gain((KV_LORA,)),
        'kv_w_ukv': nrm((KV_LORA, MLA_HEADS * (QK_NOPE + MLA_V_DIM)), KV_LORA ** -0.5),
        'kv_g_kn': gain((QK_NOPE,)),
        'kv_g_kr': gain((QK_ROPE,)),
    }


def reference(x, mem, positions, norm_mix, norm_mlp, norm_mem, w_mem_kv, g_mem_q, g_mem_k,
              w_mlp_in, w_mlp_out, a_w_in, a_conv_w, a_conv_b, a_ln_g, a_ln_b, a_w_out,
              b_w_in, b_g_qa, b_w_uq, b_g_qn, b_g_qr, b_w_out,
              kv_g_in, kv_w_dkv, kv_g_a, kv_w_ukv, kv_g_kn, kv_g_kr):
    b, s, _ = x.shape
    cos, sin = rope_tables(positions)
    k_nope = k_rope = v_sh = None
    for l in range(DEPTH):
        h = rms_norm(x, norm_mix[l])
        if l < N_A_LAYERS:
            i = l
            u = h @ a_w_in[i]
            u_conv, u_mem = u[..., :2 * CONV_CH], u[..., 2 * CONV_CH:]
            y_main = conformer_conv(u_conv, a_conv_w[i], a_conv_b[i], a_ln_g[i], a_ln_b[i])
            y_mem = memory_attention(u_mem, mem, norm_mem[l], w_mem_kv[l], g_mem_q[l], g_mem_k[l])
            x = x + jnp.concatenate([y_main, y_mem], axis=-1) @ a_w_out[i]
        else:
            j = l - N_A_LAYERS
            if j == 0:
                ckr = rms_norm(x, kv_g_in) @ kv_w_dkv
                c_kv = rms_norm(ckr[..., :KV_LORA], kv_g_a)
                k_rope = apply_rope(rms_norm(ckr[..., KV_LORA:], kv_g_kr), cos, sin)
                kv = (c_kv @ kv_w_ukv).reshape(b, s, MLA_HEADS, QK_NOPE + MLA_V_DIM)
                k_nope = rms_norm(kv[..., :QK_NOPE], kv_g_kn)
                v_sh = kv[..., QK_NOPE:]
            u = h @ b_w_in[j]
            c_q = rms_norm(u[..., :Q_LORA], b_g_qa[j])
            q = (c_q @ b_w_uq[j]).reshape(b, s, MLA_HEADS, QK_DIM)
            q_nope = rms_norm(q[..., :QK_NOPE], b_g_qn[j])
            q_rope = apply_rope(rms_norm(q[..., QK_NOPE:], b_g_qr[j]), cos[:, :, None, :], sin[:, :, None, :])
            y_main = mla_attention(q_nope, q_rope, k_nope, k_rope, v_sh)
            y_mem = memory_attention(u[..., Q_LORA:], mem, norm_mem[l], w_mem_kv[l], g_mem_q[l], g_mem_k[l])
            x = x + jnp.concatenate([y_main, y_mem], axis=-1) @ b_w_out[j]
        hm = rms_norm(x, norm_mlp[l]) @ w_mlp_in[l]
        x = x + jnp.square(jax.nn.relu(hm)) @ w_mlp_out[l]
    return x
```

```python
import functools

import jax
import jax.numpy as jnp
from jax import lax
from jax.experimental import pallas as pl
from jax.experimental.pallas import tpu as pltpu

EPS = 1e-6
ROPE_THETA = 10000.0
LANES = 128
VMEM_LIMIT_BYTES = 56 << 20
F32 = jnp.float32
BF16 = jnp.bfloat16


def _tile(n, pref, mult=LANES):
    if n <= pref:
        return n
    t = (pref // mult) * mult
    while t >= mult:
        if n % t == 0:
            return t
        t -= mult
    raise ValueError(f"no tile for {n} (pref {pref}, mult {mult})")


def _params(*semantics):
    return pltpu.CompilerParams(dimension_semantics=semantics, vmem_limit_bytes=VMEM_LIMIT_BYTES)


def _rms(x, width=None):
    width = x.shape[-1] if width is None else width
    ms = jnp.sum(x * x, axis=-1, keepdims=True) * (1.0 / width)
    return x * lax.rsqrt(ms + EPS)


def _rmsnorm_kernel(x_ref, g_ref, *o_refs):
    xn = _rms(x_ref[...])
    for i, o_ref in enumerate(o_refs):
        o_ref[...] = (xn * g_ref[i:i + 1, :]).astype(o_ref.dtype)


def rmsnorm_cast(x, gains):
    m, d = x.shape
    n = gains.shape[0]
    tm = _tile(m, 512, 8)
    return pl.pallas_call(
        _rmsnorm_kernel,
        grid=(m // tm,),
        in_specs=[pl.BlockSpec((tm, d), lambda i: (i, 0)),
                  pl.BlockSpec((n, d), lambda i: (0, 0))],
        out_specs=[pl.BlockSpec((tm, d), lambda i: (i, 0)) for _ in range(n)],
        out_shape=[jax.ShapeDtypeStruct((m, d), BF16) for _ in range(n)],
        compiler_params=_params("parallel"),
        name="rmsnorm_cast",
    )(x, gains)


def _matmul_kernel(*refs, nk, act, has_res):
    if has_res:
        a_ref, w_ref, r_ref, o_ref = refs[:4]
        scratch = refs[4:]
    else:
        a_ref, w_ref, o_ref = refs[:3]
        r_ref = None
        scratch = refs[3:]

    def finish(acc):
        if act == "relu2":
            acc = jnp.square(jnp.maximum(acc, 0.0))
        if r_ref is not None:
            acc = acc + r_ref[...]
        o_ref[...] = acc.astype(o_ref.dtype)

    part = jnp.dot(a_ref[...], w_ref[...], preferred_element_type=F32)
    if nk == 1:
        finish(part)
        return
    acc_ref = scratch[0] if scratch else o_ref
    k = pl.program_id(2)

    @pl.when(k == 0)
    def _():
        acc_ref[...] = part

    @pl.when(k > 0)
    def _():
        acc_ref[...] += part

    @pl.when(k == nk - 1)
    def _():
        finish(acc_ref[...])


def matmul(a, w, *, out_dtype, act=None, residual=None, tm=1024, tn=1024, tk=4096, name="matmul"):
    m, kdim = a.shape
    n = w.shape[1]
    tm, tn, tk = _tile(m, tm, 8), _tile(n, tn), _tile(kdim, tk)
    nk = kdim // tk
    in_specs = [pl.BlockSpec((tm, tk), lambda i, j, k: (i, k)),
                pl.BlockSpec((tk, tn), lambda i, j, k: (k, j))]
    args = [a, w]
    if residual is not None:
        in_specs.append(pl.BlockSpec((tm, tn), lambda i, j, k: (i, j)))
        args.append(residual)
    return pl.pallas_call(
        functools.partial(_matmul_kernel, nk=nk, act=act, has_res=residual is not None),
        grid=(m // tm, n // tn, nk),
        in_specs=in_specs,
        out_specs=pl.BlockSpec((tm, tn), lambda i, j, k: (i, j)),
        out_shape=jax.ShapeDtypeStruct((m, n), out_dtype),
        scratch_shapes=[pltpu.VMEM((tm, tn), F32)] if nk > 1 and out_dtype != F32 else [],
        compiler_params=_params("parallel", "parallel", "arbitrary"),
        name=name,
    )(*args)


CONV_ROWS = 16
CONV_HALO = 32


def _conv_kernel(a_ref, gate_ref, ap_ref, gatep_ref, w_ref, cb_ref, lng_ref, lnb_ref, o_ref, buf_ref,
                 *, ts, seq, cw):
    i = pl.program_id(0)
    at_seq_start = (i * ts) % seq == 0
    prev = ap_ref[...] * jax.nn.sigmoid(gatep_ref[...])
    buf_ref[0:CONV_HALO, :] = jnp.where(at_seq_start, 0.0, prev)
    buf_ref[CONV_HALO:CONV_HALO + ts, :] = a_ref[...] * jax.nn.sigmoid(gate_ref[...])
    lead = CONV_HALO - (cw - 1)
    for r0 in range(0, ts, CONV_ROWS):
        acc = buf_ref[lead + r0:lead + r0 + CONV_ROWS, :] * w_ref[0:1, :]
        for j in range(1, cw):
            acc = acc + buf_ref[lead + r0 + j:lead + r0 + j + CONV_ROWS, :] * w_ref[j:j + 1, :]
        c = acc + cb_ref[...]
        mu = jnp.mean(c, axis=-1, keepdims=True)
        cc = c - mu
        var = jnp.mean(cc * cc, axis=-1, keepdims=True)
        y = cc * lax.rsqrt(var + EPS) * lng_ref[...] + lnb_ref[...]
        o_ref[r0:r0 + CONV_ROWS, :] = (y * jax.nn.sigmoid(y)).astype(o_ref.dtype)


def conformer_conv(u, conv_w, conv_b, ln_g, ln_b, *, seq):
    m = u.shape[0]
    cw, c = conv_w.shape
    assert cw - 1 <= CONV_HALO
    ts = _tile(seq, 64, CONV_HALO)
    hb = ts // CONV_HALO
    prev_map = lambda col: (lambda i: (jnp.maximum(i * hb - 1, 0), col))
    row = lambda v: v.reshape(1, c)
    return pl.pallas_call(
        functools.partial(_conv_kernel, ts=ts, seq=seq, cw=cw),
        grid=(m // ts,),
        in_specs=[pl.BlockSpec((ts, c), lambda i: (i, 0)),
                  pl.BlockSpec((ts, c), lambda i: (i, 1)),
                  pl.BlockSpec((CONV_HALO, c), prev_map(0)),
                  pl.BlockSpec((CONV_HALO, c), prev_map(1)),
                  pl.BlockSpec((cw, c), lambda i: (0, 0)),
                  pl.BlockSpec((1, c), lambda i: (0, 0)),
                  pl.BlockSpec((1, c), lambda i: (0, 0)),
                  pl.BlockSpec((1, c), lambda i: (0, 0))],
        out_specs=pl.BlockSpec((ts, c), lambda i: (i, 0)),
        out_shape=jax.ShapeDtypeStruct((m, c), BF16),
        scratch_shapes=[pltpu.VMEM((CONV_HALO + ts, c), F32)],
        compiler_params=_params("parallel"),
        name="conformer_conv",
    )(u, u, u, u, conv_w, row(conv_b), row(ln_g), row(ln_b))


def _mem_kv_kernel(mem_ref, gn_ref, w_ref, gk_ref, o_ref, *, n_k_heads):
    j = pl.program_id(1)
    hn = (_rms(mem_ref[...]) * gn_ref[0]).astype(BF16)
    kv = jnp.dot(hn, w_ref[0], preferred_element_type=F32)
    kn = _rms(kv) * gk_ref[0]
    o_ref[0] = jnp.where(j < n_k_heads, kn, kv).astype(o_ref.dtype)


def mem_kv(mem2d, norm_mem, w_mem_kv, g_mem_k):
    nl, d, n2 = w_mem_kv.shape
    hd = g_mem_k.shape[-1]
    bm = mem2d.shape[0]
    return pl.pallas_call(
        functools.partial(_mem_kv_kernel, n_k_heads=n2 // 2 // hd),
        grid=(nl, n2 // hd),
        in_specs=[pl.BlockSpec((bm, d), lambda l, j: (0, 0)),
                  pl.BlockSpec((1, 1, d), lambda l, j: (l, 0, 0)),
                  pl.BlockSpec((1, d, hd), lambda l, j: (l, 0, j)),
                  pl.BlockSpec((1, 1, hd), lambda l, j: (l, 0, 0))],
        out_specs=pl.BlockSpec((1, bm, hd), lambda l, j: (l, 0, j)),
        out_shape=jax.ShapeDtypeStruct((nl, bm, n2), BF16),
        compiler_params=_params("parallel", "parallel"),
        name="mem_kv",
    )(mem2d, norm_mem.reshape(nl, 1, d), w_mem_kv, g_mem_k.reshape(nl, 1, hd))


def _mem_attn_kernel(q_ref, k_ref, v_ref, gq_ref, o_ref, *, n_heads, hd):
    scale = hd ** -0.5
    for h in range(n_heads):
        cols = slice(h * hd, (h + 1) * hd)
        q = (_rms(q_ref[:, cols]) * (gq_ref[...] * scale)).astype(BF16)
        s = lax.dot_general(q, k_ref[0, :, cols], (((1,), (1,)), ((), ())), preferred_element_type=F32)
        p = jnp.exp(s - jnp.max(s, axis=-1, keepdims=True))
        denom = jnp.sum(p, axis=-1, keepdims=True)
        o = jnp.dot(p.astype(BF16), v_ref[0, :, cols], preferred_element_type=F32)
        o_ref[:, cols] = (o / denom).astype(o_ref.dtype)


def mem_attention(qsrc, q_col_block, kv, layer, g_q, *, seq, mem_len):
    m = qsrc.shape[0]
    w = kv.shape[-1] // 2
    hd = g_q.shape[-1]
    ts = _tile(seq, 512, 8)
    tiles_per_seq = seq // ts
    return pl.pallas_call(
        functools.partial(_mem_attn_kernel, n_heads=w // hd, hd=hd),
        grid=(m // ts,),
        in_specs=[pl.BlockSpec((ts, w), lambda i: (i, q_col_block)),
                  pl.BlockSpec((1, mem_len, w), lambda i: (layer, i // tiles_per_seq, 0)),
                  pl.BlockSpec((1, mem_len, w), lambda i: (layer, i // tiles_per_seq, 1)),
                  pl.BlockSpec((1, hd), lambda i: (0, 0))],
        out_specs=pl.BlockSpec((ts, w), lambda i: (i, 0)),
        out_shape=jax.ShapeDtypeStruct((m, w), BF16),
        compiler_params=_params("parallel"),
        name="mem_attention",
    )(qsrc, kv, kv, g_q.reshape(1, hd))


def _rope(t, cos_ref, sin_ref):
    half = (LANES // 2) // 2
    partner = pltpu.roll(t, LANES - half, 1) + pltpu.roll(t, half, 1)
    return t * cos_ref[...] + partner * sin_ref[...]


def _dkv_kernel(h_ref, w_ref, ga_ref, gr_ref, cos_ref, sin_ref, ckv_ref, kr_ref, *, lora, rope):
    ckr = jnp.dot(h_ref[...], w_ref[...], preferred_element_type=F32)
    ckv_ref[...] = (_rms(ckr[:, :lora]) * ga_ref[...]).astype(ckv_ref.dtype)
    kr = _rms(ckr[:, lora:], rope) * gr_ref[...]
    kr_ref[...] = _rope(kr, cos_ref, sin_ref).astype(kr_ref.dtype)


def kv_down(h, w_dkv_pad, g_a, g_kr_pad, cos_t, sin_t, *, rope):
    m, d = h.shape
    lora = g_a.shape[-1]
    tm = _tile(m, 1024, 8)
    full = lambda shape: pl.BlockSpec(shape, lambda i: (0, 0))
    rows = lambda width: pl.BlockSpec((tm, width), lambda i: (i, 0))
    return pl.pallas_call(
        functools.partial(_dkv_kernel, lora=lora, rope=rope),
        grid=(m // tm,),
        in_specs=[rows(d), full(w_dkv_pad.shape), full((1, lora)), full((1, LANES)), rows(LANES), rows(LANES)],
        out_specs=[rows(lora), rows(LANES)],
        out_shape=[jax.ShapeDtypeStruct((m, lora), BF16), jax.ShapeDtypeStruct((m, LANES), BF16)],
        compiler_params=_params("parallel"),
        name="kv_down",
    )(h, w_dkv_pad, g_a.reshape(1, lora), g_kr_pad, cos_t, sin_t)


def _ukv_kernel(c_ref, wk_ref, wv_ref, kr_ref, gk_ref, k_ref, v_ref, *, heads, nope):
    c = c_ref[...]
    kn = jnp.dot(c, wk_ref[...], preferred_element_type=F32)
    for h in range(heads):
        base = h * 2 * nope
        k_ref[:, base:base + nope] = (_rms(kn[:, h * nope:(h + 1) * nope]) * gk_ref[...]).astype(k_ref.dtype)
        k_ref[:, base + nope:base + 2 * nope] = kr_ref[...]
    v_ref[...] = jnp.dot(c, wv_ref[...], preferred_element_type=F32).astype(v_ref.dtype)


def kv_up(c_kv, w_kn, w_v, kr, g_kn, *, n_heads):
    m, lora = c_kv.shape
    nope = g_kn.shape[-1]
    dv = w_v.shape[1] // n_heads
    assert nope == LANES and dv == LANES
    hb = 2 if n_heads % 2 == 0 else 1
    tm = _tile(m, 1024, 8)
    return pl.pallas_call(
        functools.partial(_ukv_kernel, heads=hb, nope=nope),
        grid=(m // tm, n_heads // hb),
        in_specs=[pl.BlockSpec((tm, lora), lambda i, j: (i, 0)),
                  pl.BlockSpec((lora, hb * nope), lambda i, j: (0, j)),
                  pl.BlockSpec((lora, hb * dv), lambda i, j: (0, j)),
                  pl.BlockSpec((tm, LANES), lambda i, j: (i, 0)),
                  pl.BlockSpec((1, nope), lambda i, j: (0, 0))],
        out_specs=[pl.BlockSpec((tm, hb * 2 * nope), lambda i, j: (i, j)),
                   pl.BlockSpec((tm, hb * dv), lambda i, j: (i, j))],
        out_shape=[jax.ShapeDtypeStruct((m, n_heads * 2 * nope), BF16),
                   jax.ShapeDtypeStruct((m, n_heads * dv), BF16)],
        compiler_params=_params("parallel", "parallel"),
        name="kv_up",
    )(c_kv, w_kn, w_v, kr, g_kn.reshape(1, nope))


def _q_kernel(u_ref, ga_ref, w_ref, gn_ref, gr_ref, cos_ref, sin_ref, q_ref, cq_ref, *, heads, nope, rope, scale):
    @pl.when(pl.program_id(1) == 0)
    def _():
        cq_ref[...] = (_rms(u_ref[...]) * ga_ref[...]).astype(cq_ref.dtype)

    q = jnp.dot(cq_ref[...], w_ref[...], preferred_element_type=F32)
    for h in range(heads):
        base = h * 2 * nope
        qn = _rms(q[:, base:base + nope]) * (gn_ref[...] * scale)
        q_ref[:, base:base + nope] = qn.astype(q_ref.dtype)
        qr = _rms(q[:, base + nope:base + 2 * nope], rope) * (gr_ref[...] * scale)
        q_ref[:, base + nope:base + 2 * nope] = _rope(qr, cos_ref, sin_ref).astype(q_ref.dtype)


def q_proj(u, g_qa, w_q_pad, g_qn, g_qr_pad, cos_t, sin_t, *, n_heads, rope, scale):
    m = u.shape[0]
    lora = g_qa.shape[-1]
    nope = g_qn.shape[-1]
    assert nope == LANES
    hb = 2 if n_heads % 2 == 0 else 1
    tm = _tile(m, 1024, 8)
    return pl.pallas_call(
        functools.partial(_q_kernel, heads=hb, nope=nope, rope=rope, scale=scale),
        grid=(m // tm, n_heads // hb),
        in_specs=[pl.BlockSpec((tm, lora), lambda i, j: (i, 0)),
                  pl.BlockSpec((1, lora), lambda i, j: (0, 0)),
                  pl.BlockSpec((lora, hb * 2 * nope), lambda i, j: (0, j)),
                  pl.BlockSpec((1, nope), lambda i, j: (0, 0)),
                  pl.BlockSpec((1, LANES), lambda i, j: (0, 0)),
                  pl.BlockSpec((tm, LANES), lambda i, j: (i, 0)),
                  pl.BlockSpec((tm, LANES), lambda i, j: (i, 0))],
        out_specs=pl.BlockSpec((tm, hb * 2 * nope), lambda i, j: (i, j)),
        out_shape=jax.ShapeDtypeStruct((m, n_heads * 2 * nope), BF16),
        scratch_shapes=[pltpu.VMEM((tm, lora), BF16)],
        compiler_params=_params("parallel", "arbitrary"),
        name="q_proj",
    )(u, g_qa.reshape(1, lora), w_q_pad, g_qn.reshape(1, nope), g_qr_pad, cos_t, sin_t)


def _mla_kernel(q_ref, k_ref, v_ref, o_ref, m_ref, l_ref, acc_ref, *, tq):
    i = pl.program_id(2)
    q = q_ref[...]
    m_ref[...] = jnp.full(m_ref.shape, -jnp.inf, F32)
    l_ref[...] = jnp.zeros(l_ref.shape, F32)
    acc_ref[...] = jnp.zeros(acc_ref.shape, F32)

    def step(j, masked):
        start = pl.multiple_of(j * tq, tq)
        s = lax.dot_general(q, k_ref[pl.ds(start, tq), :], (((1,), (1,)), ((), ())),
                            preferred_element_type=F32)
        if masked:
            row = lax.broadcasted_iota(jnp.int32, s.shape, 0)
            col = lax.broadcasted_iota(jnp.int32, s.shape, 1)
            s = jnp.where(col <= row, s, -jnp.inf)
        m_prev = m_ref[...]
        m_new = jnp.maximum(m_prev, jnp.max(s, axis=-1, keepdims=True))
        alpha = jnp.exp(m_prev - m_new)
        p = jnp.exp(s - m_new)
        l_ref[...] = alpha * l_ref[...] + jnp.sum(p, axis=-1, keepdims=True)
        acc_ref[...] = alpha * acc_ref[...] + jnp.dot(p.astype(BF16), v_ref[pl.ds(start, tq), :],
                                                      preferred_element_type=F32)
        m_ref[...] = m_new

    def body(j, carry):
        step(j, masked=False)
        return carry

    lax.fori_loop(0, i, body, 0)
    step(i, masked=True)
    o_ref[...] = (acc_ref[...] / l_ref[...]).astype(o_ref.dtype)


def mla_attention(q, k, v, *, batch, seq, n_heads):
    m = q.shape[0]
    dqk = q.shape[1] // n_heads
    dv = v.shape[1] // n_heads
    tq = _tile(seq, 512, 8)
    nq = seq // tq
    return pl.pallas_call(
        functools.partial(_mla_kernel, tq=tq),
        grid=(batch, n_heads, nq),
        in_specs=[pl.BlockSpec((tq, dqk), lambda b, h, i: (b * nq + i, h)),
                  pl.BlockSpec((seq, dqk), lambda b, h, i: (b, h)),
                  pl.BlockSpec((seq, dv), lambda b, h, i: (b, h))],
        out_specs=pl.BlockSpec((tq, dv), lambda b, h, i: (b * nq + i, h)),
        out_shape=jax.ShapeDtypeStruct((m, n_heads * dv), BF16),
        scratch_shapes=[pltpu.VMEM((tq, 1), F32), pltpu.VMEM((tq, 1), F32), pltpu.VMEM((tq, dv), F32)],
        compiler_params=_params("parallel", "parallel", "arbitrary"),
        name="mla_attention",
    )(q, k, v)


def _mlp(x, gain, w_in, w_out):
    h, = rmsnorm_cast(x, gain.reshape(1, -1))
    hm = matmul(h, w_in, out_dtype=BF16, act="relu2", name="mlp_in")
    return matmul(hm, w_out, out_dtype=F32, residual=x, tm=1024, tn=2048, tk=1024, name="mlp_out")


def kernel(x, mem, positions, norm_mix, norm_mlp, norm_mem, w_mem_kv, g_mem_q, g_mem_k, w_mlp_in, w_mlp_out,
           a_w_in, a_conv_w, a_conv_b, a_ln_g, a_ln_b, a_w_out, b_w_in, b_g_qa, b_w_uq, b_g_qn, b_g_qr, b_w_out,
           kv_g_in, kv_w_dkv, kv_g_a, kv_w_ukv, kv_g_kn, kv_g_kr):
    batch, seq, d = x.shape
    m = batch * seq
    mem_len = mem.shape[1]
    mem_w = w_mem_kv.shape[-1] // 2
    conv_ch = a_conv_w.shape[-1]
    q_lora = b_g_qa.shape[-1]
    kv_lora = kv_g_a.shape[-1]
    nope, rope = b_g_qn.shape[-1], b_g_qr.shape[-1]
    n_heads = b_w_uq.shape[-1] // (nope + rope)
    dv = kv_w_ukv.shape[-1] // n_heads - nope
    assert 2 * rope == LANES and nope == LANES and dv == LANES
    assert (2 * conv_ch) % mem_w == 0 and q_lora % mem_w == 0
    assert b_w_in.shape[0] == 1 and a_w_in.shape[0] == 1 and norm_mix.shape[0] == 2

    cast = lambda w: w.astype(BF16)
    w_uq = b_w_uq[0].reshape(q_lora, n_heads, nope + rope)
    w_q_pad = cast(jnp.pad(w_uq, ((0, 0), (0, 0), (0, 2 * nope - nope - rope))).reshape(q_lora, n_heads * 2 * nope))
    w_ukv = kv_w_ukv.reshape(kv_lora, n_heads, nope + dv)
    w_kn = cast(w_ukv[:, :, :nope].reshape(kv_lora, n_heads * nope))
    w_v = cast(w_ukv[:, :, nope:].reshape(kv_lora, n_heads * dv))
    w_dkv_pad = cast(jnp.pad(kv_w_dkv, ((0, 0), (0, LANES - rope))))
    pad_gain = lambda g: jnp.pad(g.reshape(1, rope), ((0, 0), (0, LANES - rope)))

    inv_freq = ROPE_THETA ** (-jnp.arange(0, rope, 2, dtype=F32) / rope)
    ang = positions.astype(F32).reshape(m, 1) * inv_freq
    zeros = jnp.zeros((m, LANES - rope), F32)
    cos_t = jnp.concatenate([jnp.cos(ang), jnp.cos(ang), zeros], axis=-1)
    sin_t = jnp.concatenate([-jnp.sin(ang), jnp.sin(ang), zeros], axis=-1)

    x = x.reshape(m, d)
    kv_mem = mem_kv(mem.reshape(batch * mem_len, d), norm_mem, cast(w_mem_kv), g_mem_k)

    h, = rmsnorm_cast(x, norm_mix[0:1])
    u = matmul(h, cast(a_w_in[0]), out_dtype=F32, name="a_in_proj")
    y_main = conformer_conv(u, a_conv_w[0], a_conv_b[0], a_ln_g[0], a_ln_b[0], seq=seq)
    y_mem = mem_attention(u, 2 * conv_ch // mem_w, kv_mem, 0, g_mem_q[0], seq=seq, mem_len=mem_len)
    y = jnp.concatenate([y_main, y_mem], axis=-1)
    x = matmul(y, cast(a_w_out[0]), out_dtype=F32, residual=x, name="a_out_proj")
    x = _mlp(x, norm_mlp[0], cast(w_mlp_in[0]), cast(w_mlp_out[0]))

    h_kv, h = rmsnorm_cast(x, jnp.stack([kv_g_in, norm_mix[1]]))
    c_kv, k_rope = kv_down(h_kv, w_dkv_pad, kv_g_a, pad_gain(kv_g_kr), cos_t, sin_t, rope=rope)
    k_all, v_all = kv_up(c_kv, w_kn, w_v, k_rope, kv_g_kn, n_heads=n_heads)
    u = matmul(h, cast(b_w_in[0]), out_dtype=F32, name="b_in_proj")
    q_all = q_proj(u, b_g_qa[0], w_q_pad, b_g_qn[0], pad_gain(b_g_qr[0]), cos_t, sin_t,
                   n_heads=n_heads, rope=rope, scale=(nope + rope) ** -0.5)
    y_main = mla_attention(q_all, k_all, v_all, batch=batch, seq=seq, n_heads=n_heads)
    y_mem = mem_attention(u, q_lora // mem_w, kv_mem, 1, g_mem_q[1], seq=seq, mem_len=mem_len)
    y = jnp.concatenate([y_main, y_mem], axis=-1)
    x = matmul(y, cast(b_w_out[0]), out_dtype=F32, residual=x, name="b_out_proj")
    x = _mlp(x, norm_mlp[1], cast(w_mlp_in[1]), cast(w_mlp_out[1]))
    return x.reshape(batch, seq, d)
```

```python
import functools
import math

import jax
import jax.numpy as jnp
from jax import lax
from jax.experimental import pallas as pl
from jax.experimental.pallas import tpu as pltpu

EPS = 1e-6
ROPE_THETA = 10000.0
LANES = 128
VMEM_LIMIT_BYTES = 56 << 20
F32 = jnp.float32
BF16 = jnp.bfloat16


def _tile(n, pref, mult=LANES):
    if n <= pref:
        return n
    t = (pref // mult) * mult
    while t >= mult:
        if n % t == 0:
            return t
        t -= mult
    raise ValueError(f"no tile for {n} (pref {pref}, mult {mult})")


def _params(*semantics):
    return pltpu.CompilerParams(dimension_semantics=semantics, vmem_limit_bytes=VMEM_LIMIT_BYTES)


def _rms(x, width=None):
    width = x.shape[-1] if width is None else width
    ms = jnp.sum(x * x, axis=-1, keepdims=True) * (1.0 / width)
    return x * lax.rsqrt(ms + EPS)


def _rmsnorm_kernel(x_ref, g_ref, *o_refs):
    xn = _rms(x_ref[...])
    for i, o_ref in enumerate(o_refs):
        o_ref[...] = (xn * g_ref[i:i + 1, :]).astype(o_ref.dtype)


def rmsnorm_cast(x, gains):
    m, d = x.shape
    n = gains.shape[0]
    tm = _tile(m, 512, 8)
    return pl.pallas_call(
        _rmsnorm_kernel,
        grid=(m // tm,),
        in_specs=[pl.BlockSpec((tm, d), lambda i: (i, 0)),
                  pl.BlockSpec((n, d), lambda i: (0, 0))],
        out_specs=[pl.BlockSpec((tm, d), lambda i: (i, 0)) for _ in range(n)],
        out_shape=[jax.ShapeDtypeStruct((m, d), BF16) for _ in range(n)],
        compiler_params=_params("parallel"),
        name="rmsnorm_cast",
    )(x, gains)


def _matmul_kernel(*refs, nk, act, has_res):
    if has_res:
        a_ref, w_ref, r_ref, o_ref = refs[:4]
        scratch = refs[4:]
    else:
        a_ref, w_ref, o_ref = refs[:3]
        r_ref = None
        scratch = refs[3:]

    def finish(acc):
        if act == "relu2":
            acc = jnp.square(jnp.maximum(acc, 0.0))
        if r_ref is not None:
            acc = acc + r_ref[...]
        o_ref[...] = acc.astype(o_ref.dtype)

    product = lambda: jnp.dot(a_ref[...], w_ref[...], preferred_element_type=F32)
    if nk == 1:
        finish(product())
        return
    k = pl.program_id(2)
    if not scratch:
        assert act is None and o_ref.dtype == F32

        @pl.when(k == 0)
        def _():
            o_ref[...] = jnp.zeros(o_ref.shape, F32) if r_ref is None else r_ref[...]

        o_ref[...] += product()
        return
    acc_ref, = scratch

    @pl.when(k == 0)
    def _():
        acc_ref[...] = jnp.zeros(acc_ref.shape, F32)

    acc_ref[...] += product()

    @pl.when(k == nk - 1)
    def _():
        finish(acc_ref[...])


def matmul(a, w, *, out_dtype, act=None, residual=None, tm=1024, tn=1024, tk=4096, name="matmul"):
    m, kdim = a.shape
    n = w.shape[1]
    tm, tn, tk = _tile(m, tm, 8), _tile(n, tn), _tile(kdim, tk)
    nk = kdim // tk
    in_specs = [pl.BlockSpec((tm, tk), lambda i, j, k: (i, k)),
                pl.BlockSpec((tk, tn), lambda i, j, k: (k, j))]
    args = [a, w]
    if residual is not None:
        in_specs.append(pl.BlockSpec((tm, tn), lambda i, j, k: (i, j)))
        args.append(residual)
    return pl.pallas_call(
        functools.partial(_matmul_kernel, nk=nk, act=act, has_res=residual is not None),
        grid=(m // tm, n // tn, nk),
        in_specs=in_specs,
        out_specs=pl.BlockSpec((tm, tn), lambda i, j, k: (i, j)),
        out_shape=jax.ShapeDtypeStruct((m, n), out_dtype),
        scratch_shapes=[pltpu.VMEM((tm, tn), F32)] if nk > 1 and out_dtype != F32 else [],
        compiler_params=_params("parallel", "parallel", "arbitrary"),
        name=name,
    )(*args)


SUBLANES = 8
CONV_ROWS = 16
CONV_HALO = 32
CONV_TS = 256
CONV_LANES = 512


def _sigmoid(x):
    return 0.5 * jnp.tanh(0.5 * x) + 0.5


def _conv_kernel(a_ref, gate_ref, w_ref, cb_ref, lng_ref, lnb_ref, o_ref, buf_ref, c_ref, *, ts, seq, cw, lc):
    i = pl.program_id(0)
    at_seq_start = (i * ts) % seq == 0

    @pl.when(at_seq_start)
    def _():
        buf_ref[0:CONV_HALO, :] = jnp.zeros((CONV_HALO, buf_ref.shape[1]), F32)

    @pl.when(jnp.logical_not(at_seq_start))
    def _():
        buf_ref[0:CONV_HALO, :] = buf_ref[ts:ts + CONV_HALO, :]

    buf_ref[CONV_HALO:CONV_HALO + ts, :] = a_ref[...] * _sigmoid(gate_ref[...])
    lead = CONV_HALO - (cw - 1)
    n_ch = a_ref.shape[1]

    def chunk(r, carry):
        r0 = pl.multiple_of(r * CONV_ROWS, CONV_ROWS)
        for c0 in range(0, n_ch, lc):
            cols = slice(c0, c0 + lc)
            acc = None
            for phase in range(SUBLANES):
                taps = [j for j in range(cw) if (lead + j) % SUBLANES == phase]
                if not taps:
                    continue
                rows = CONV_ROWS + (SUBLANES if phase else 0)
                part = None
                for j in taps:
                    base = (lead + j) // SUBLANES * SUBLANES
                    x = buf_ref[pl.ds(r0 + base, rows), cols].reshape(rows // SUBLANES, SUBLANES, lc)
                    term = x * w_ref[j, :, cols][None]
                    part = term if part is None else part + term
                part = part.reshape(rows, lc)
                piece = part[phase:phase + CONV_ROWS] if phase else part
                acc = piece if acc is None else acc + piece
            c_ref[pl.ds(r0, CONV_ROWS), cols] = acc + cb_ref[:, cols]
        c = c_ref[pl.ds(r0, CONV_ROWS), :]
        mu = jnp.mean(c, axis=-1, keepdims=True)
        cc = c - mu
        var = jnp.mean(cc * cc, axis=-1, keepdims=True)
        y = cc * lax.rsqrt(var + EPS) * lng_ref[...] + lnb_ref[...]
        o_ref[pl.ds(r0, CONV_ROWS), :] = (y * _sigmoid(y)).astype(o_ref.dtype)
        return carry

    lax.fori_loop(0, ts // CONV_ROWS, chunk, 0)


def conformer_conv(u, conv_w, conv_b, ln_g, ln_b, *, seq):
    m = u.shape[0]
    cw, c = conv_w.shape
    assert cw - 1 <= CONV_HALO
    ts = _tile(seq, CONV_TS, CONV_HALO)
    lc = _tile(c, CONV_LANES)
    row = lambda v: v.reshape(1, c)
    return pl.pallas_call(
        functools.partial(_conv_kernel, ts=ts, seq=seq, cw=cw, lc=lc),
        grid=(m // ts,),
        in_specs=[pl.BlockSpec((ts, c), lambda i: (i, 0)),
                  pl.BlockSpec((ts, c), lambda i: (i, 1)),
                  pl.BlockSpec((cw, SUBLANES, c), lambda i: (0, 0, 0)),
                  pl.BlockSpec((1, c), lambda i: (0, 0)),
                  pl.BlockSpec((1, c), lambda i: (0, 0)),
                  pl.BlockSpec((1, c), lambda i: (0, 0))],
        out_specs=pl.BlockSpec((ts, c), lambda i: (i, 0)),
        out_shape=jax.ShapeDtypeStruct((m, c), BF16),
        scratch_shapes=[pltpu.VMEM((CONV_HALO + ts, c), F32), pltpu.VMEM((ts, c), F32)],
        compiler_params=_params("arbitrary"),
        name="conformer_conv",
    )(u, u, jnp.broadcast_to(conv_w[:, None, :], (cw, SUBLANES, c)), row(conv_b), row(ln_g), row(ln_b))


def _mem_kv_kernel(mem_ref, gn_ref, w_ref, gk_ref, o_ref, *, n_k_heads):
    j = pl.program_id(1)
    hn = (_rms(mem_ref[...]) * gn_ref[0]).astype(BF16)
    kv = jnp.dot(hn, w_ref[0], preferred_element_type=F32)
    kn = _rms(kv) * gk_ref[0]
    o_ref[0] = jnp.where(j < n_k_heads, kn, kv).astype(o_ref.dtype)


def mem_kv(mem2d, norm_mem, w_mem_kv, g_mem_k):
    nl, d, n2 = w_mem_kv.shape
    hd = g_mem_k.shape[-1]
    bm = mem2d.shape[0]
    return pl.pallas_call(
        functools.partial(_mem_kv_kernel, n_k_heads=n2 // 2 // hd),
        grid=(nl, n2 // hd),
        in_specs=[pl.BlockSpec((bm, d), lambda l, j: (0, 0)),
                  pl.BlockSpec((1, 1, d), lambda l, j: (l, 0, 0)),
                  pl.BlockSpec((1, d, hd), lambda l, j: (l, 0, j)),
                  pl.BlockSpec((1, 1, hd), lambda l, j: (l, 0, 0))],
        out_specs=pl.BlockSpec((1, bm, hd), lambda l, j: (l, 0, j)),
        out_shape=jax.ShapeDtypeStruct((nl, bm, n2), BF16),
        compiler_params=_params("parallel", "parallel"),
        name="mem_kv",
    )(mem2d, norm_mem.reshape(nl, 1, d), w_mem_kv, g_mem_k.reshape(nl, 1, hd))


def _mem_attn_kernel(q_ref, k_ref, v_ref, gq_ref, o_ref, *, n_heads, hd):
    scale = hd ** -0.5
    for h in range(n_heads):
        cols = slice(h * hd, (h + 1) * hd)
        q = (_rms(q_ref[:, cols]) * (gq_ref[...] * scale)).astype(BF16)
        s = lax.dot_general(q, k_ref[0, :, cols], (((1,), (1,)), ((), ())), preferred_element_type=F32)
        p = jnp.exp(s - jnp.max(s, axis=-1, keepdims=True))
        denom = jnp.sum(p, axis=-1, keepdims=True)
        o = jnp.dot(p.astype(BF16), v_ref[0, :, cols], preferred_element_type=F32)
        o_ref[:, cols] = (o / denom).astype(o_ref.dtype)


def mem_attention(qsrc, q_col_block, kv, layer, g_q, *, seq, mem_len):
    m = qsrc.shape[0]
    w = kv.shape[-1] // 2
    hd = g_q.shape[-1]
    ts = _tile(seq, 512, 8)
    tiles_per_seq = seq // ts
    return pl.pallas_call(
        functools.partial(_mem_attn_kernel, n_heads=w // hd, hd=hd),
        grid=(m // ts,),
        in_specs=[pl.BlockSpec((ts, w), lambda i: (i, q_col_block)),
                  pl.BlockSpec((1, mem_len, w), lambda i: (layer, i // tiles_per_seq, 0)),
                  pl.BlockSpec((1, mem_len, w), lambda i: (layer, i // tiles_per_seq, 1)),
                  pl.BlockSpec((1, hd), lambda i: (0, 0))],
        out_specs=pl.BlockSpec((ts, w), lambda i: (i, 0)),
        out_shape=jax.ShapeDtypeStruct((m, w), BF16),
        compiler_params=_params("parallel"),
        name="mem_attention",
    )(qsrc, kv, kv, g_q.reshape(1, hd))


def _rope(t, cos_ref, sin_ref):
    half = (LANES // 2) // 2
    partner = pltpu.roll(t, LANES - half, 1) + pltpu.roll(t, half, 1)
    return t * cos_ref[...] + partner * sin_ref[...]


def _dkv_kernel(h_ref, w_ref, ga_ref, gr_ref, cos_ref, sin_ref, ckv_ref, kr_ref, *, lora, rope):
    ckr = jnp.dot(h_ref[...], w_ref[...], preferred_element_type=F32)
    ckv_ref[...] = (_rms(ckr[:, :lora]) * ga_ref[...]).astype(ckv_ref.dtype)
    kr = _rms(ckr[:, lora:], rope) * gr_ref[...]
    kr_ref[...] = _rope(kr, cos_ref, sin_ref).astype(kr_ref.dtype)


def kv_down(h, w_dkv_pad, g_a, g_kr_pad, cos_t, sin_t, *, rope):
    m, d = h.shape
    lora = g_a.shape[-1]
    tm = _tile(m, 1024, 8)
    full = lambda shape: pl.BlockSpec(shape, lambda i: (0, 0))
    rows = lambda width: pl.BlockSpec((tm, width), lambda i: (i, 0))
    return pl.pallas_call(
        functools.partial(_dkv_kernel, lora=lora, rope=rope),
        grid=(m // tm,),
        in_specs=[rows(d), full(w_dkv_pad.shape), full((1, lora)), full((1, LANES)), rows(LANES), rows(LANES)],
        out_specs=[rows(lora), rows(LANES)],
        out_shape=[jax.ShapeDtypeStruct((m, lora), BF16), jax.ShapeDtypeStruct((m, LANES), BF16)],
        compiler_params=_params("parallel"),
        name="kv_down",
    )(h, w_dkv_pad, g_a.reshape(1, lora), g_kr_pad, cos_t, sin_t)


def _ukv_kernel(c_ref, wk_ref, wv_ref, kr_ref, gk_ref, k_ref, v_ref, *, heads, nope):
    c = c_ref[...]
    kn = jnp.dot(c, wk_ref[...], preferred_element_type=F32)
    for h in range(heads):
        base = h * 2 * nope
        k_ref[:, base:base + nope] = (_rms(kn[:, h * nope:(h + 1) * nope]) * gk_ref[...]).astype(k_ref.dtype)
        k_ref[:, base + nope:base + 2 * nope] = kr_ref[...]
    v_ref[...] = jnp.dot(c, wv_ref[...], preferred_element_type=F32).astype(v_ref.dtype)


def kv_up(c_kv, w_kn, w_v, kr, g_kn, *, n_heads):
    m, lora = c_kv.shape
    nope = g_kn.shape[-1]
    dv = w_v.shape[1] // n_heads
    assert nope == LANES and dv == LANES
    hb = 2 if n_heads % 2 == 0 else 1
    tm = _tile(m, 1024, 8)
    return pl.pallas_call(
        functools.partial(_ukv_kernel, heads=hb, nope=nope),
        grid=(m // tm, n_heads // hb),
        in_specs=[pl.BlockSpec((tm, lora), lambda i, j: (i, 0)),
                  pl.BlockSpec((lora, hb * nope), lambda i, j: (0, j)),
                  pl.BlockSpec((lora, hb * dv), lambda i, j: (0, j)),
                  pl.BlockSpec((tm, LANES), lambda i, j: (i, 0)),
                  pl.BlockSpec((1, nope), lambda i, j: (0, 0))],
        out_specs=[pl.BlockSpec((tm, hb * 2 * nope), lambda i, j: (i, j)),
                   pl.BlockSpec((tm, hb * dv), lambda i, j: (i, j))],
        out_shape=[jax.ShapeDtypeStruct((m, n_heads * 2 * nope), BF16),
                   jax.ShapeDtypeStruct((m, n_heads * dv), BF16)],
        compiler_params=_params("parallel", "parallel"),
        name="kv_up",
    )(c_kv, w_kn, w_v, kr, g_kn.reshape(1, nope))


def _q_kernel(u_ref, ga_ref, w_ref, gn_ref, gr_ref, cos_ref, sin_ref, q_ref, cq_ref, *, heads, nope, rope, scale):
    @pl.when(pl.program_id(1) == 0)
    def _():
        cq_ref[...] = (_rms(u_ref[...]) * ga_ref[...]).astype(cq_ref.dtype)

    q = jnp.dot(cq_ref[...], w_ref[...], preferred_element_type=F32)
    for h in range(heads):
        base = h * 2 * nope
        qn = _rms(q[:, base:base + nope]) * (gn_ref[...] * scale)
        q_ref[:, base:base + nope] = qn.astype(q_ref.dtype)
        qr = _rms(q[:, base + nope:base + 2 * nope], rope) * (gr_ref[...] * scale)
        q_ref[:, base + nope:base + 2 * nope] = _rope(qr, cos_ref, sin_ref).astype(q_ref.dtype)


def q_proj(u, g_qa, w_q_pad, g_qn, g_qr_pad, cos_t, sin_t, *, n_heads, rope, scale):
    m = u.shape[0]
    lora = g_qa.shape[-1]
    nope = g_qn.shape[-1]
    assert nope == LANES
    hb = 2 if n_heads % 2 == 0 else 1
    tm = _tile(m, 1024, 8)
    return pl.pallas_call(
        functools.partial(_q_kernel, heads=hb, nope=nope, rope=rope, scale=scale),
        grid=(m // tm, n_heads // hb),
        in_specs=[pl.BlockSpec((tm, lora), lambda i, j: (i, 0)),
                  pl.BlockSpec((1, lora), lambda i, j: (0, 0)),
                  pl.BlockSpec((lora, hb * 2 * nope), lambda i, j: (0, j)),
                  pl.BlockSpec((1, nope), lambda i, j: (0, 0)),
                  pl.BlockSpec((1, LANES), lambda i, j: (0, 0)),
                  pl.BlockSpec((tm, LANES), lambda i, j: (i, 0)),
                  pl.BlockSpec((tm, LANES), lambda i, j: (i, 0))],
        out_specs=pl.BlockSpec((tm, hb * 2 * nope), lambda i, j: (i, j)),
        out_shape=jax.ShapeDtypeStruct((m, n_heads * 2 * nope), BF16),
        scratch_shapes=[pltpu.VMEM((tm, lora), BF16)],
        compiler_params=_params("parallel", "arbitrary"),
        name="q_proj",
    )(u, g_qa.reshape(1, lora), w_q_pad, g_qn.reshape(1, nope), g_qr_pad, cos_t, sin_t)


ATT_TQ = 1024
ATT_TK = 512
ATT_GROUP = 4


def _mla_kernel(q_ref, k_ref, v_ref, o_ref, m_ref, l_ref, acc_ref, *, tq, tk, group):
    i = pl.program_id(2)
    m_ref[...] = jnp.full(m_ref.shape, -jnp.inf, F32)
    l_ref[...] = jnp.zeros(l_ref.shape, F32)
    acc_ref[...] = jnp.zeros(acc_ref.shape, F32)
    chunks = tk // LANES

    def sub_block(start, diag_offset=None):
        s = lax.dot_general(q_ref[...], k_ref[pl.ds(start, tk), :], (((1,), (1,)), ((), ())),
                            preferred_element_type=F32)
        if diag_offset is not None:
            row = lax.broadcasted_iota(jnp.int32, s.shape, 0)
            col = lax.broadcasted_iota(jnp.int32, s.shape, 1) + diag_offset
            s = jnp.where(col <= row, s, -jnp.inf)
        m_prev = m_ref[...]
        m_new = jnp.maximum(m_prev, jnp.max(s, axis=-1, keepdims=True))
        alpha = jnp.exp2(m_prev - m_new)
        p = [jnp.exp2(s[:, c * LANES:(c + 1) * LANES] - m_new) for c in range(chunks)]
        l_ref[...] = alpha * l_ref[...] + functools.reduce(lambda a, b: a + b, p)
        pv = jnp.dot(jnp.concatenate(p, axis=1).astype(BF16), v_ref[pl.ds(start, tk), :],
                     preferred_element_type=F32)
        acc_ref[...] = alpha * acc_ref[...] + pv
        m_ref[...] = m_new

    per_tile = tq // tk
    n_off = i * per_tile
    n_groups = n_off // group

    def body(g, carry):
        for t in range(group):
            sub_block(pl.multiple_of((g * group + t) * tk, tk))
        return carry

    lax.fori_loop(0, n_groups, body, 0)
    step = per_tile
    while group % step:
        step -= 1
    for rem in range(step, group, step):
        @pl.when(n_off % group == rem)
        def _(rem=rem):
            for t in range(rem):
                sub_block(pl.multiple_of((n_groups * group + t) * tk, tk))
    for t in range(per_tile):
        sub_block(pl.multiple_of(i * tq + t * tk, tk), diag_offset=t * tk)
    denom = jnp.sum(l_ref[...], axis=-1, keepdims=True)
    o_ref[...] = (acc_ref[...] / denom).astype(o_ref.dtype)


def mla_attention(q, k, v, *, batch, seq, n_heads):
    m = q.shape[0]
    dqk = q.shape[1] // n_heads
    dv = v.shape[1] // n_heads
    assert dv == LANES
    tq = _tile(seq, ATT_TQ, ATT_TK)
    tk = min(ATT_TK, tq)
    nq = seq // tq
    return pl.pallas_call(
        functools.partial(_mla_kernel, tq=tq, tk=tk, group=ATT_GROUP),
        grid=(batch, n_heads, nq),
        in_specs=[pl.BlockSpec((tq, dqk), lambda b, h, i: (b * nq + i, h)),
                  pl.BlockSpec((seq, dqk), lambda b, h, i: (b, h)),
                  pl.BlockSpec((seq, dv), lambda b, h, i: (b, h))],
        out_specs=pl.BlockSpec((tq, dv), lambda b, h, i: (b * nq + i, h)),
        out_shape=jax.ShapeDtypeStruct((m, n_heads * dv), BF16),
        scratch_shapes=[pltpu.VMEM((tq, LANES), F32), pltpu.VMEM((tq, LANES), F32), pltpu.VMEM((tq, dv), F32)],
        compiler_params=_params("parallel", "parallel", "arbitrary"),
        name="mla_attention",
    )(q, k, v)


def _mlp(x, gain, w_in, w_out):
    h, = rmsnorm_cast(x, gain.reshape(1, -1))
    hm = matmul(h, w_in, out_dtype=BF16, act="relu2", name="mlp_in")
    return matmul(hm, w_out, out_dtype=F32, residual=x, tm=1024, tn=1024, tk=2048, name="mlp_out")


def kernel(x, mem, positions, norm_mix, norm_mlp, norm_mem, w_mem_kv, g_mem_q, g_mem_k, w_mlp_in, w_mlp_out,
           a_w_in, a_conv_w, a_conv_b, a_ln_g, a_ln_b, a_w_out, b_w_in, b_g_qa, b_w_uq, b_g_qn, b_g_qr, b_w_out,
           kv_g_in, kv_w_dkv, kv_g_a, kv_w_ukv, kv_g_kn, kv_g_kr):
    batch, seq, d = x.shape
    m = batch * seq
    mem_len = mem.shape[1]
    mem_w = w_mem_kv.shape[-1] // 2
    conv_ch = a_conv_w.shape[-1]
    q_lora = b_g_qa.shape[-1]
    kv_lora = kv_g_a.shape[-1]
    nope, rope = b_g_qn.shape[-1], b_g_qr.shape[-1]
    n_heads = b_w_uq.shape[-1] // (nope + rope)
    dv = kv_w_ukv.shape[-1] // n_heads - nope
    assert 2 * rope == LANES and nope == LANES and dv == LANES
    assert (2 * conv_ch) % mem_w == 0 and q_lora % mem_w == 0
    assert b_w_in.shape[0] == 1 and a_w_in.shape[0] == 1 and norm_mix.shape[0] == 2

    cast = lambda w: w.astype(BF16)
    w_uq = b_w_uq[0].reshape(q_lora, n_heads, nope + rope)
    w_q_pad = cast(jnp.pad(w_uq, ((0, 0), (0, 0), (0, 2 * nope - nope - rope))).reshape(q_lora, n_heads * 2 * nope))
    w_ukv = kv_w_ukv.reshape(kv_lora, n_heads, nope + dv)
    w_kn = cast(w_ukv[:, :, :nope].reshape(kv_lora, n_heads * nope))
    w_v = cast(w_ukv[:, :, nope:].reshape(kv_lora, n_heads * dv))
    w_dkv_pad = cast(jnp.pad(kv_w_dkv, ((0, 0), (0, LANES - rope))))
    pad_gain = lambda g: jnp.pad(g.reshape(1, rope), ((0, 0), (0, LANES - rope)))

    inv_freq = ROPE_THETA ** (-jnp.arange(0, rope, 2, dtype=F32) / rope)
    ang = positions.astype(F32).reshape(m, 1) * inv_freq
    zeros = jnp.zeros((m, LANES - rope), F32)
    cos_t = jnp.concatenate([jnp.cos(ang), jnp.cos(ang), zeros], axis=-1)
    sin_t = jnp.concatenate([-jnp.sin(ang), jnp.sin(ang), zeros], axis=-1)

    x = x.reshape(m, d)
    kv_mem = mem_kv(mem.reshape(batch * mem_len, d), norm_mem, cast(w_mem_kv), g_mem_k)

    h, = rmsnorm_cast(x, norm_mix[0:1])
    u = matmul(h, cast(a_w_in[0]), out_dtype=F32, name="a_in_proj")
    y_main = conformer_conv(u, a_conv_w[0], a_conv_b[0], a_ln_g[0], a_ln_b[0], seq=seq)
    y_mem = mem_attention(u, 2 * conv_ch // mem_w, kv_mem, 0, g_mem_q[0], seq=seq, mem_len=mem_len)
    y = jnp.concatenate([y_main, y_mem], axis=-1)
    x = matmul(y, cast(a_w_out[0]), out_dtype=F32, residual=x, name="a_out_proj")
    x = _mlp(x, norm_mlp[0], cast(w_mlp_in[0]), cast(w_mlp_out[0]))

    h_kv, h = rmsnorm_cast(x, jnp.stack([kv_g_in, norm_mix[1]]))
    c_kv, k_rope = kv_down(h_kv, w_dkv_pad, kv_g_a, pad_gain(kv_g_kr), cos_t, sin_t, rope=rope)
    k_all, v_all = kv_up(c_kv, w_kn, w_v, k_rope, kv_g_kn, n_heads=n_heads)
    u = matmul(h, cast(b_w_in[0]), out_dtype=F32, name="b_in_proj")
    q_all = q_proj(u, b_g_qa[0], w_q_pad, b_g_qn[0], pad_gain(b_g_qr[0]), cos_t, sin_t,
                   n_heads=n_heads, rope=rope, scale=(nope + rope) ** -0.5 * math.log2(math.e))
    y_main = mla_attention(q_all, k_all, v_all, batch=batch, seq=seq, n_heads=n_heads)
    y_mem = mem_attention(u, q_lora // mem_w, kv_mem, 1, g_mem_q[1], seq=seq, mem_len=mem_len)
    y = jnp.concatenate([y_main, y_mem], axis=-1)
    x = matmul(y, cast(b_w_out[0]), out_dtype=F32, residual=x, name="b_out_proj")
    x = _mlp(x, norm_mlp[1], cast(w_mlp_in[1]), cast(w_mlp_out[1]))
    return x.reshape(batch, seq, d)
```

```python
import functools
import math

import jax
import jax.numpy as jnp
from jax import lax
from jax.experimental import pallas as pl
from jax.experimental.pallas import tpu as pltpu

EPS = 1e-6
ROPE_THETA = 10000.0
LANES = 128
VMEM_LIMIT_BYTES = 56 << 20
F32 = jnp.float32
BF16 = jnp.bfloat16


def _tile(n, pref, mult=LANES):
    if n <= pref:
        return n
    t = (pref // mult) * mult
    while t >= mult:
        if n % t == 0:
            return t
        t -= mult
    raise ValueError(f"no tile for {n} (pref {pref}, mult {mult})")


def _params(*semantics):
    return pltpu.CompilerParams(dimension_semantics=semantics, vmem_limit_bytes=VMEM_LIMIT_BYTES)


def _rms(x, width=None):
    width = x.shape[-1] if width is None else width
    ms = jnp.sum(x * x, axis=-1, keepdims=True) * (1.0 / width)
    return x * lax.rsqrt(ms + EPS)


def _rmsnorm_kernel(x_ref, g_ref, *o_refs):
    xn = _rms(x_ref[...])
    for i, o_ref in enumerate(o_refs):
        o_ref[...] = (xn * g_ref[i:i + 1, :]).astype(o_ref.dtype)


def rmsnorm_cast(x, gains):
    m, d = x.shape
    n = gains.shape[0]
    tm = _tile(m, 512, 8)
    return pl.pallas_call(
        _rmsnorm_kernel,
        grid=(m // tm,),
        in_specs=[pl.BlockSpec((tm, d), lambda i: (i, 0)),
                  pl.BlockSpec((n, d), lambda i: (0, 0))],
        out_specs=[pl.BlockSpec((tm, d), lambda i: (i, 0)) for _ in range(n)],
        out_shape=[jax.ShapeDtypeStruct((m, d), BF16) for _ in range(n)],
        compiler_params=_params("parallel"),
        name="rmsnorm_cast",
    )(x, gains)


def _matmul_kernel(*refs, n_parts, act, has_res):
    a_refs, w_refs = refs[:n_parts], refs[n_parts:2 * n_parts]
    r_ref = refs[2 * n_parts] if has_res else None
    o_ref = refs[-1]
    acc = None
    for a_ref, w_ref in zip(a_refs, w_refs):
        part = jnp.dot(a_ref[...], w_ref[...].astype(BF16), preferred_element_type=F32)
        acc = part if acc is None else acc + part
    if act == "relu2":
        acc = jnp.square(jnp.maximum(acc, 0.0))
    if r_ref is not None:
        acc = acc + r_ref[...]
    o_ref[...] = acc.astype(o_ref.dtype)


def matmul(a_parts, w, *, out_dtype, act=None, residual=None, tm, tn, single_buffer_a=False, layer=None, name):
    m = a_parts[0].shape[0]
    n = w.shape[-1]
    tm, tn = _tile(m, tm, 8), _tile(n, tn)
    a_mode = dict(pipeline_mode=pl.Buffered(1)) if single_buffer_a else {}
    in_specs, w_specs, offset = [], [], 0
    for a in a_parts:
        kp = a.shape[1]
        assert offset % kp == 0
        in_specs.append(pl.BlockSpec((tm, kp), lambda i, j: (i, 0), **a_mode))
        if layer is None:
            w_specs.append(pl.BlockSpec((kp, tn), lambda i, j, blk=offset // kp: (blk, j)))
        else:
            w_specs.append(pl.BlockSpec((None, kp, tn), lambda i, j, blk=offset // kp: (layer, blk, j)))
        offset += kp
    assert offset == w.shape[-2]
    in_specs += w_specs
    args = list(a_parts) + [w] * len(a_parts)
    if residual is not None:
        in_specs.append(pl.BlockSpec((tm, tn), lambda i, j: (i, j)))
        args.append(residual)
    return pl.pallas_call(
        functools.partial(_matmul_kernel, n_parts=len(a_parts), act=act, has_res=residual is not None),
        grid=(m // tm, n // tn),
        in_specs=in_specs,
        out_specs=pl.BlockSpec((tm, tn), lambda i, j: (i, j)),
        out_shape=jax.ShapeDtypeStruct((m, n), out_dtype),
        compiler_params=_params("parallel", "arbitrary"),
        name=name,
    )(*args)


def _matmul_ksplit_kernel(a_ref, w_ref, r_ref, o_ref):
    @pl.when(pl.program_id(2) == 0)
    def _():
        o_ref[...] = r_ref[...]

    o_ref[...] += jnp.dot(a_ref[...], w_ref[...], preferred_element_type=F32)


def matmul_ksplit(a, w, residual, *, tm, tn, tk, name):
    m, kdim = a.shape
    n = w.shape[1]
    tm, tn, tk = _tile(m, tm, 8), _tile(n, tn), _tile(kdim, tk)
    return pl.pallas_call(
        _matmul_ksplit_kernel,
        grid=(m // tm, n // tn, kdim // tk),
        in_specs=[pl.BlockSpec((tm, tk), lambda i, j, k: (i, k)),
                  pl.BlockSpec((tk, tn), lambda i, j, k: (k, j)),
                  pl.BlockSpec((tm, tn), lambda i, j, k: (i, j))],
        out_specs=pl.BlockSpec((tm, tn), lambda i, j, k: (i, j)),
        out_shape=jax.ShapeDtypeStruct((m, n), F32),
        compiler_params=_params("parallel", "parallel", "arbitrary"),
        name=name,
    )(a, w, residual)


SUBLANES = 8
CONV_ROWS = 16
CONV_HALO = 32
CONV_TS = 256
CONV_LANES = 512


def _sigmoid(x):
    return 0.5 * jnp.tanh(0.5 * x) + 0.5


def _conv_kernel(a_ref, gate_ref, w_ref, cb_ref, lng_ref, lnb_ref, o_ref, buf_ref, c_ref, *, ts, seq, cw, lc):
    i = pl.program_id(0)
    at_seq_start = (i * ts) % seq == 0

    @pl.when(at_seq_start)
    def _():
        buf_ref[0:CONV_HALO, :] = jnp.zeros((CONV_HALO, buf_ref.shape[1]), F32)

    @pl.when(jnp.logical_not(at_seq_start))
    def _():
        buf_ref[0:CONV_HALO, :] = buf_ref[ts:ts + CONV_HALO, :]

    buf_ref[CONV_HALO:CONV_HALO + ts, :] = a_ref[...] * _sigmoid(gate_ref[...])
    lead = CONV_HALO - (cw - 1)
    n_ch = a_ref.shape[1]

    def chunk(r, carry):
        r0 = pl.multiple_of(r * CONV_ROWS, CONV_ROWS)
        for c0 in range(0, n_ch, lc):
            cols = slice(c0, c0 + lc)
            acc = None
            for phase in range(SUBLANES):
                taps = [j for j in range(cw) if (lead + j) % SUBLANES == phase]
                if not taps:
                    continue
                rows = CONV_ROWS + (SUBLANES if phase else 0)
                part = None
                for j in taps:
                    base = (lead + j) // SUBLANES * SUBLANES
                    x = buf_ref[pl.ds(r0 + base, rows), cols].reshape(rows // SUBLANES, SUBLANES, lc)
                    term = x * w_ref[j, :, cols][None]
                    part = term if part is None else part + term
                part = part.reshape(rows, lc)
                piece = part[phase:phase + CONV_ROWS] if phase else part
                acc = piece if acc is None else acc + piece
            c_ref[pl.ds(r0, CONV_ROWS), cols] = acc + cb_ref[:, cols]
        c = c_ref[pl.ds(r0, CONV_ROWS), :]
        mu = jnp.mean(c, axis=-1, keepdims=True)
        cc = c - mu
        var = jnp.mean(cc * cc, axis=-1, keepdims=True)
        y = cc * lax.rsqrt(var + EPS) * lng_ref[...] + lnb_ref[...]
        o_ref[pl.ds(r0, CONV_ROWS), :] = (y * _sigmoid(y)).astype(o_ref.dtype)
        return carry

    lax.fori_loop(0, ts // CONV_ROWS, chunk, 0)


def conformer_conv(u, conv_w, conv_b, ln_g, ln_b, *, seq):
    m = u.shape[0]
    cw, c = conv_w.shape
    assert cw - 1 <= CONV_HALO
    ts = _tile(seq, CONV_TS, CONV_HALO)
    lc = _tile(c, CONV_LANES)
    row = lambda v: v.reshape(1, c)
    return pl.pallas_call(
        functools.partial(_conv_kernel, ts=ts, seq=seq, cw=cw, lc=lc),
        grid=(m // ts,),
        in_specs=[pl.BlockSpec((ts, c), lambda i: (i, 0)),
                  pl.BlockSpec((ts, c), lambda i: (i, 1)),
                  pl.BlockSpec((cw, SUBLANES, c), lambda i: (0, 0, 0)),
                  pl.BlockSpec((1, c), lambda i: (0, 0)),
                  pl.BlockSpec((1, c), lambda i: (0, 0)),
                  pl.BlockSpec((1, c), lambda i: (0, 0))],
        out_specs=pl.BlockSpec((ts, c), lambda i: (i, 0)),
        out_shape=jax.ShapeDtypeStruct((m, c), BF16),
        scratch_shapes=[pltpu.VMEM((CONV_HALO + ts, c), F32), pltpu.VMEM((ts, c), F32)],
        compiler_params=_params("arbitrary"),
        name="conformer_conv",
    )(u, u, jnp.broadcast_to(conv_w[:, None, :], (cw, SUBLANES, c)), row(conv_b), row(ln_g), row(ln_b))


def _mem_kv_kernel(mem_ref, gn_ref, w_ref, gk_ref, o_ref, *, n_k_heads):
    j = pl.program_id(1)
    hn = (_rms(mem_ref[...]) * gn_ref[0]).astype(BF16)
    kv = jnp.dot(hn, w_ref[0], preferred_element_type=F32)
    kn = _rms(kv) * gk_ref[0]
    o_ref[0] = jnp.where(j < n_k_heads, kn, kv).astype(o_ref.dtype)


def mem_kv(mem2d, norm_mem, w_mem_kv, g_mem_k):
    nl, d, n2 = w_mem_kv.shape
    hd = g_mem_k.shape[-1]
    bm = mem2d.shape[0]
    return pl.pallas_call(
        functools.partial(_mem_kv_kernel, n_k_heads=n2 // 2 // hd),
        grid=(nl, n2 // hd),
        in_specs=[pl.BlockSpec((bm, d), lambda l, j: (0, 0)),
                  pl.BlockSpec((1, 1, d), lambda l, j: (l, 0, 0)),
                  pl.BlockSpec((1, d, hd), lambda l, j: (l, 0, j)),
                  pl.BlockSpec((1, 1, hd), lambda l, j: (l, 0, 0))],
        out_specs=pl.BlockSpec((1, bm, hd), lambda l, j: (l, 0, j)),
        out_shape=jax.ShapeDtypeStruct((nl, bm, n2), BF16),
        compiler_params=_params("parallel", "parallel"),
        name="mem_kv",
    )(mem2d, norm_mem.reshape(nl, 1, d), w_mem_kv, g_mem_k.reshape(nl, 1, hd))


def _mem_attn_kernel(q_ref, k_ref, v_ref, gq_ref, o_ref, *, n_heads, hd):
    scale = hd ** -0.5
    for h in range(n_heads):
        cols = slice(h * hd, (h + 1) * hd)
        q = (_rms(q_ref[:, cols]) * (gq_ref[...] * scale)).astype(BF16)
        s = lax.dot_general(q, k_ref[0, :, cols], (((1,), (1,)), ((), ())), preferred_element_type=F32)
        p = jnp.exp(s - jnp.max(s, axis=-1, keepdims=True))
        denom = jnp.sum(p, axis=-1, keepdims=True)
        o = jnp.dot(p.astype(BF16), v_ref[0, :, cols], preferred_element_type=F32)
        o_ref[:, cols] = (o / denom).astype(o_ref.dtype)


def mem_attention(qsrc, q_col_block, kv, layer, g_q, *, seq, mem_len):
    m = qsrc.shape[0]
    w = kv.shape[-1] // 2
    hd = g_q.shape[-1]
    ts = _tile(seq, 512, 8)
    tiles_per_seq = seq // ts
    return pl.pallas_call(
        functools.partial(_mem_attn_kernel, n_heads=w // hd, hd=hd),
        grid=(m // ts,),
        in_specs=[pl.BlockSpec((ts, w), lambda i: (i, q_col_block)),
                  pl.BlockSpec((1, mem_len, w), lambda i: (layer, i // tiles_per_seq, 0)),
                  pl.BlockSpec((1, mem_len, w), lambda i: (layer, i // tiles_per_seq, 1)),
                  pl.BlockSpec((1, hd), lambda i: (0, 0))],
        out_specs=pl.BlockSpec((ts, w), lambda i: (i, 0)),
        out_shape=jax.ShapeDtypeStruct((m, w), BF16),
        compiler_params=_params("parallel"),
        name="mem_attention",
    )(qsrc, kv, kv, g_q.reshape(1, hd))


def _rope(t, cos_ref, sin_ref):
    partner = pltpu.roll(t, LANES // 4, 1)
    return t * cos_ref[...] + partner * sin_ref[...]


def _dkv_kernel(h_ref, w_ref, ga_ref, gr_ref, cos_ref, sin_ref, ckv_ref, kr_ref, *, lora):
    ckr = jnp.dot(h_ref[...], w_ref[...], preferred_element_type=F32)
    ckv_ref[...] = (_rms(ckr[:, :lora]) * ga_ref[...]).astype(ckv_ref.dtype)
    kr = _rms(ckr[:, lora:]) * gr_ref[...]
    kr_ref[...] = _rope(kr, cos_ref, sin_ref).astype(kr_ref.dtype)


def kv_down(h, w_dkv_pad, g_a, g_kr_pad, cos_t, sin_t):
    m, d = h.shape
    lora = g_a.shape[-1]
    tm = _tile(m, 1024, 8)
    full = lambda shape: pl.BlockSpec(shape, lambda i: (0, 0))
    rows = lambda width: pl.BlockSpec((tm, width), lambda i: (i, 0))
    return pl.pallas_call(
        functools.partial(_dkv_kernel, lora=lora),
        grid=(m // tm,),
        in_specs=[rows(d), full(w_dkv_pad.shape), full((1, lora)), full((1, LANES)), rows(LANES), rows(LANES)],
        out_specs=[rows(lora), rows(LANES)],
        out_shape=[jax.ShapeDtypeStruct((m, lora), BF16), jax.ShapeDtypeStruct((m, LANES), BF16)],
        compiler_params=_params("parallel"),
        name="kv_down",
    )(h, w_dkv_pad, g_a.reshape(1, lora), g_kr_pad, cos_t, sin_t)


HEAD_PAIR = 2


def _ukv_kernel(c_ref, wk_ref, wv_ref, kr_ref, gk_ref, k_ref, v_ref, *, heads, nope):
    c = c_ref[...]
    kr = kr_ref[...]
    width = HEAD_PAIR * nope
    for pair in range(heads // HEAD_PAIR):
        kn = jnp.dot(c, wk_ref[:, pair * width:(pair + 1) * width], preferred_element_type=F32)
        for h in range(HEAD_PAIR):
            base = (pair * HEAD_PAIR + h) * 2 * nope
            k_ref[:, base:base + nope] = (_rms(kn[:, h * nope:(h + 1) * nope]) * gk_ref[...]).astype(k_ref.dtype)
            k_ref[:, base + nope:base + 2 * nope] = kr
        cols = slice(pair * width, (pair + 1) * width)
        v_ref[:, cols] = jnp.dot(c, wv_ref[:, cols], preferred_element_type=F32).astype(v_ref.dtype)


def kv_up(c_kv, w_kn, w_v, kr, g_kn, *, n_heads):
    m, lora = c_kv.shape
    nope = g_kn.shape[-1]
    dv = w_v.shape[1] // n_heads
    assert nope == LANES and dv == LANES and n_heads % HEAD_PAIR == 0
    tm = _tile(m, 512, 8)
    whole = lambda shape: pl.BlockSpec(shape, lambda i: (0, 0))
    rows = lambda width: pl.BlockSpec((tm, width), lambda i: (i, 0))
    return pl.pallas_call(
        functools.partial(_ukv_kernel, heads=n_heads, nope=nope),
        grid=(m // tm,),
        in_specs=[rows(lora), whole(w_kn.shape), whole(w_v.shape), rows(LANES), whole((1, nope))],
        out_specs=[rows(n_heads * 2 * nope), rows(n_heads * dv)],
        out_shape=[jax.ShapeDtypeStruct((m, n_heads * 2 * nope), BF16),
                   jax.ShapeDtypeStruct((m, n_heads * dv), BF16)],
        compiler_params=_params("parallel"),
        name="kv_up",
    )(c_kv, w_kn, w_v, kr, g_kn.reshape(1, nope))


def _q_kernel(u_ref, ga_ref, w_ref, gn_ref, gr_ref, cos_ref, sin_ref, q_ref, cq_ref, *, heads, nope, scale):
    cq_ref[...] = (_rms(u_ref[...]) * ga_ref[...]).astype(cq_ref.dtype)
    width = HEAD_PAIR * 2 * nope
    for pair in range(heads // HEAD_PAIR):
        q = jnp.dot(cq_ref[...], w_ref[:, pair * width:(pair + 1) * width], preferred_element_type=F32)
        for h in range(HEAD_PAIR):
            base = h * 2 * nope
            out = pair * width + base
            qn = _rms(q[:, base:base + nope]) * (gn_ref[...] * scale)
            q_ref[:, out:out + nope] = qn.astype(q_ref.dtype)
            qr = _rms(q[:, base + nope:base + 2 * nope]) * (gr_ref[...] * scale)
            q_ref[:, out + nope:out + 2 * nope] = _rope(qr, cos_ref, sin_ref).astype(q_ref.dtype)


def q_proj(u, g_qa, w_q_pad, g_qn, g_qr_pad, cos_t, sin_t, *, n_heads, scale):
    m = u.shape[0]
    lora = g_qa.shape[-1]
    nope = g_qn.shape[-1]
    assert nope == LANES and n_heads % HEAD_PAIR == 0
    tm = _tile(m, 512, 8)
    whole = lambda shape: pl.BlockSpec(shape, lambda i: (0, 0))
    rows = lambda width: pl.BlockSpec((tm, width), lambda i: (i, 0))
    return pl.pallas_call(
        functools.partial(_q_kernel, heads=n_heads, nope=nope, scale=scale),
        grid=(m // tm,),
        in_specs=[rows(lora), whole((1, lora)), whole(w_q_pad.shape), whole((1, nope)), whole((1, LANES)),
                  rows(LANES), rows(LANES)],
        out_specs=rows(n_heads * 2 * nope),
        out_shape=jax.ShapeDtypeStruct((m, n_heads * 2 * nope), BF16),
        scratch_shapes=[pltpu.VMEM((tm, lora), BF16)],
        compiler_params=_params("parallel"),
        name="q_proj",
    )(u, g_qa.reshape(1, lora), w_q_pad, g_qn.reshape(1, nope), g_qr_pad, cos_t, sin_t)


ATT_TQ = 2048
ATT_TK = 512
ATT_GROUP = 4


def _mla_kernel(q_ref, k_ref, v_ref, o_ref, m_ref, l_ref, acc_ref, *, tq, tk, group):
    i = pl.program_id(2)
    m_ref[...] = jnp.full(m_ref.shape, -jnp.inf, F32)
    l_ref[...] = jnp.zeros(l_ref.shape, F32)
    acc_ref[...] = jnp.zeros(acc_ref.shape, F32)
    chunks = tk // LANES

    def sub_block(start, diag_offset=None):
        rows = slice(0 if diag_offset is None else diag_offset, tq)
        s = lax.dot_general(q_ref[rows, :], k_ref[pl.ds(start, tk), :], (((1,), (1,)), ((), ())),
                            preferred_element_type=F32)
        if diag_offset is not None:
            row = lax.broadcasted_iota(jnp.int32, s.shape, 0)
            col = lax.broadcasted_iota(jnp.int32, s.shape, 1)
            s = jnp.where(col <= row, s, -jnp.inf)
        m_prev = m_ref[rows, :]
        m_new = jnp.maximum(m_prev, jnp.max(s, axis=-1, keepdims=True))
        alpha = jnp.exp2(m_prev - m_new)
        p = [jnp.exp2(s[:, c * LANES:(c + 1) * LANES] - m_new) for c in range(chunks)]
        l_ref[rows, :] = alpha * l_ref[rows, :] + functools.reduce(lambda a, b: a + b, p)
        pv = jnp.dot(jnp.concatenate(p, axis=1).astype(BF16), v_ref[pl.ds(start, tk), :],
                     preferred_element_type=F32)
        acc_ref[rows, :] = alpha * acc_ref[rows, :] + pv
        m_ref[rows, :] = m_new

    per_tile = tq // tk
    n_off = i * per_tile
    n_groups = n_off // group

    def body(g, carry):
        for t in range(group):
            sub_block(pl.multiple_of((g * group + t) * tk, tk))
        return carry

    lax.fori_loop(0, n_groups, body, 0)
    step = per_tile
    while group % step:
        step -= 1
    for rem in range(step, group, step):
        @pl.when(n_off % group == rem)
        def _(rem=rem):
            for t in range(rem):
                sub_block(pl.multiple_of((n_groups * group + t) * tk, tk))
    for t in range(per_tile):
        sub_block(pl.multiple_of(i * tq + t * tk, tk), diag_offset=t * tk)
    denom = jnp.sum(l_ref[...], axis=-1, keepdims=True)
    o_ref[...] = (acc_ref[...] / denom).astype(o_ref.dtype)


def mla_attention(q, k, v, *, batch, seq, n_heads):
    m = q.shape[0]
    dqk = q.shape[1] // n_heads
    dv = v.shape[1] // n_heads
    assert dv == LANES
    tq = _tile(seq, ATT_TQ, ATT_TK)
    tk = min(ATT_TK, tq)
    nq = seq // tq
    return pl.pallas_call(
        functools.partial(_mla_kernel, tq=tq, tk=tk, group=ATT_GROUP),
        grid=(batch, n_heads, nq),
        in_specs=[pl.BlockSpec((tq, dqk), lambda b, h, i: (b * nq + i, h)),
                  pl.BlockSpec((seq, dqk), lambda b, h, i: (b, h)),
                  pl.BlockSpec((seq, dv), lambda b, h, i: (b, h))],
        out_specs=pl.BlockSpec((tq, dv), lambda b, h, i: (b * nq + i, h)),
        out_shape=jax.ShapeDtypeStruct((m, n_heads * dv), BF16),
        scratch_shapes=[pltpu.VMEM((tq, LANES), F32), pltpu.VMEM((tq, LANES), F32), pltpu.VMEM((tq, dv), F32)],
        compiler_params=_params("parallel", "parallel", "arbitrary"),
        name="mla_attention",
    )(q, k, v)


def _wide(a_parts, w_f32, **kw):
    return matmul(a_parts, w_f32, tm=2048, tn=512, single_buffer_a=True, **kw)


def _mlp(x, gain, w_in_f32, layer, w_out):
    h, = rmsnorm_cast(x, gain.reshape(1, -1))
    hm = _wide([h], w_in_f32, layer=layer, out_dtype=BF16, act="relu2", name="mlp_in")
    return matmul_ksplit(hm, w_out, x, tm=1024, tn=1024, tk=4096, name="mlp_out")


def kernel(x, mem, positions, norm_mix, norm_mlp, norm_mem, w_mem_kv, g_mem_q, g_mem_k, w_mlp_in, w_mlp_out,
           a_w_in, a_conv_w, a_conv_b, a_ln_g, a_ln_b, a_w_out, b_w_in, b_g_qa, b_w_uq, b_g_qn, b_g_qr, b_w_out,
           kv_g_in, kv_w_dkv, kv_g_a, kv_w_ukv, kv_g_kn, kv_g_kr):
    batch, seq, d = x.shape
    m = batch * seq
    mem_len = mem.shape[1]
    mem_w = w_mem_kv.shape[-1] // 2
    conv_ch = a_conv_w.shape[-1]
    q_lora = b_g_qa.shape[-1]
    kv_lora = kv_g_a.shape[-1]
    nope, rope = b_g_qn.shape[-1], b_g_qr.shape[-1]
    n_heads = b_w_uq.shape[-1] // (nope + rope)
    dv = kv_w_ukv.shape[-1] // n_heads - nope
    assert 2 * rope == LANES and nope == LANES and dv == LANES
    assert (2 * conv_ch) % mem_w == 0 and q_lora % mem_w == 0
    assert b_w_in.shape[0] == 1 and a_w_in.shape[0] == 1 and norm_mix.shape[0] == 2

    cast = lambda w: w.astype(BF16)
    w_uq = b_w_uq[0].reshape(q_lora, n_heads, nope + rope)
    w_q_pad = cast(jnp.concatenate([w_uq, w_uq[:, :, nope:]], axis=-1).reshape(q_lora, n_heads * 2 * nope))
    w_ukv = kv_w_ukv.reshape(kv_lora, n_heads, nope + dv)
    w_kn = cast(w_ukv[:, :, :nope].reshape(kv_lora, n_heads * nope))
    w_v = cast(w_ukv[:, :, nope:].reshape(kv_lora, n_heads * dv))
    w_dkv_pad = cast(jnp.concatenate([kv_w_dkv, kv_w_dkv[:, kv_lora:]], axis=-1))
    pad_gain = lambda g: jnp.concatenate([g, g]).reshape(1, 2 * rope)

    inv_freq = ROPE_THETA ** (-jnp.arange(0, rope, 2, dtype=F32) / rope)
    ang = positions.astype(F32).reshape(m, 1) * inv_freq
    zeros = jnp.zeros((m, LANES - rope), F32)
    cos_t = jnp.concatenate([jnp.cos(ang), jnp.cos(ang), zeros], axis=-1)
    sin_t = jnp.concatenate([-jnp.sin(ang), jnp.sin(ang), zeros], axis=-1)

    x = x.reshape(m, d)
    kv_mem = mem_kv(mem.reshape(batch * mem_len, d), norm_mem, cast(w_mem_kv), g_mem_k)

    out_proj = functools.partial(matmul, out_dtype=F32, tm=1024, tn=512)
    h, = rmsnorm_cast(x, norm_mix[0:1])
    u = _wide([h], a_w_in[0], out_dtype=F32, name="a_in_proj")
    y_main = conformer_conv(u, a_conv_w[0], a_conv_b[0], a_ln_g[0], a_ln_b[0], seq=seq)
    y_mem = mem_attention(u, 2 * conv_ch // mem_w, kv_mem, 0, g_mem_q[0], seq=seq, mem_len=mem_len)
    x = out_proj([y_main, y_mem], cast(a_w_out[0]), residual=x, name="a_out_proj")
    x = _mlp(x, norm_mlp[0], w_mlp_in, 0, cast(w_mlp_out[0]))

    h_kv, h = rmsnorm_cast(x, jnp.stack([kv_g_in, norm_mix[1]]))
    c_kv, k_rope = kv_down(h_kv, w_dkv_pad, kv_g_a, pad_gain(kv_g_kr), cos_t, sin_t)
    k_all, v_all = kv_up(c_kv, w_kn, w_v, k_rope, kv_g_kn, n_heads=n_heads)
    u = _wide([h], b_w_in[0], out_dtype=F32, name="b_in_proj")
    q_all = q_proj(u, b_g_qa[0], w_q_pad, b_g_qn[0], pad_gain(b_g_qr[0]), cos_t, sin_t,
                   n_heads=n_heads, scale=(nope + rope) ** -0.5 * math.log2(math.e))
    y_main = mla_attention(q_all, k_all, v_all, batch=batch, seq=seq, n_heads=n_heads)
    y_mem = mem_attention(u, q_lora // mem_w, kv_mem, 1, g_mem_q[1], seq=seq, mem_len=mem_len)
    x = out_proj([y_main, y_mem], cast(b_w_out[0]), residual=x, name="b_out_proj")
    x = _mlp(x, norm_mlp[1], w_mlp_in, 1, cast(w_mlp_out[1]))
    return x.reshape(batch, seq, d)
```

```python
import functools
import math

import jax
import jax.numpy as jnp
from jax import lax
from jax.experimental import pallas as pl
from jax.experimental.pallas import tpu as pltpu

EPS = 1e-6
ROPE_THETA = 10000.0
LANES = 128
VMEM_LIMIT_BYTES = 56 << 20
F32 = jnp.float32
BF16 = jnp.bfloat16


def _tile(n, pref, mult=LANES):
    if n <= pref:
        return n
    t = (pref // mult) * mult
    while t >= mult:
        if n % t == 0:
            return t
        t -= mult
    raise ValueError(f"no tile for {n} (pref {pref}, mult {mult})")


def _params(*semantics):
    return pltpu.CompilerParams(dimension_semantics=semantics, vmem_limit_bytes=VMEM_LIMIT_BYTES)


def _rms(x, width=None):
    width = x.shape[-1] if width is None else width
    ms = jnp.sum(x * x, axis=-1, keepdims=True) * (1.0 / width)
    return x * lax.rsqrt(ms + EPS)


def _rmsnorm_kernel(x_ref, g_ref, *o_refs):
    xn = _rms(x_ref[...])
    for i, o_ref in enumerate(o_refs):
        o_ref[...] = (xn * g_ref[i:i + 1, :]).astype(o_ref.dtype)


def rmsnorm_cast(x, gains):
    m, d = x.shape
    n = gains.shape[0]
    tm = _tile(m, 512, 8)
    return pl.pallas_call(
        _rmsnorm_kernel,
        grid=(m // tm,),
        in_specs=[pl.BlockSpec((tm, d), lambda i: (i, 0)),
                  pl.BlockSpec((n, d), lambda i: (0, 0))],
        out_specs=[pl.BlockSpec((tm, d), lambda i: (i, 0)) for _ in range(n)],
        out_shape=[jax.ShapeDtypeStruct((m, d), BF16) for _ in range(n)],
        compiler_params=_params("parallel"),
        name="rmsnorm_cast",
    )(x, gains)


def _matmul_kernel(*refs, n_parts, act, has_res, has_side):
    a_refs, w_refs = refs[:n_parts], refs[n_parts:2 * n_parts]
    r_ref = refs[2 * n_parts] if has_res else None
    if has_side:
        side_in, o_ref, side_out = refs[-3:]
        side_out[...] = side_in[...].astype(side_out.dtype)
    else:
        o_ref = refs[-1]
    acc = None
    for a_ref, w_ref in zip(a_refs, w_refs):
        part = jnp.dot(a_ref[...], w_ref[...].astype(BF16), preferred_element_type=F32)
        acc = part if acc is None else acc + part
    if act == "relu2":
        acc = jnp.square(jnp.maximum(acc, 0.0))
    if r_ref is not None:
        acc = acc + r_ref[...]
    o_ref[...] = acc.astype(o_ref.dtype)


def matmul(a_parts, w, *, out_dtype, act=None, residual=None, tm, tn, single_buffer_a=False, layer=None,
           side_cast=None, name):
    m = a_parts[0].shape[0]
    n = w.shape[-1]
    tm, tn = _tile(m, tm, 8), _tile(n, tn)
    a_mode = dict(pipeline_mode=pl.Buffered(1)) if single_buffer_a else {}
    in_specs, w_specs, offset = [], [], 0
    for a in a_parts:
        kp = a.shape[1]
        assert offset % kp == 0
        in_specs.append(pl.BlockSpec((tm, kp), lambda i, j: (i, 0), **a_mode))
        if layer is None:
            w_specs.append(pl.BlockSpec((kp, tn), lambda i, j, blk=offset // kp: (blk, j)))
        else:
            w_specs.append(pl.BlockSpec((None, kp, tn), lambda i, j, blk=offset // kp: (layer, blk, j)))
        offset += kp
    assert offset == w.shape[-2]
    in_specs += w_specs
    args = list(a_parts) + [w] * len(a_parts)
    if residual is not None:
        in_specs.append(pl.BlockSpec((tm, tn), lambda i, j: (i, j)))
        args.append(residual)
    out_specs = [pl.BlockSpec((tm, tn), lambda i, j: (i, j))]
    out_shape = [jax.ShapeDtypeStruct((m, n), out_dtype)]
    if side_cast is not None:
        src, src_layer = side_cast
        nj = n // tn
        slab = src.shape[1] // ((m // tm) * nj)
        in_specs.append(pl.BlockSpec((None, slab, src.shape[2]), lambda i, j: (src_layer, i * nj + j, 0)))
        args.append(src)
        out_specs.append(pl.BlockSpec((slab, src.shape[2]), lambda i, j: (i * nj + j, 0)))
        out_shape.append(jax.ShapeDtypeStruct(src.shape[1:], BF16))
    outs = pl.pallas_call(
        functools.partial(_matmul_kernel, n_parts=len(a_parts), act=act, has_res=residual is not None,
                          has_side=side_cast is not None),
        grid=(m // tm, n // tn),
        in_specs=in_specs,
        out_specs=out_specs,
        out_shape=out_shape,
        compiler_params=_params("parallel", "arbitrary"),
        name=name,
    )(*args)
    return outs if side_cast is not None else outs[0]


def side_cast_fits(rows, m, n, tm, tn):
    steps = (m // _tile(m, tm, 8)) * (n // _tile(n, tn))
    return rows % steps == 0 and (rows // steps) % 16 == 0


def _matmul_ksplit_kernel(a_ref, w_ref, r_ref, o_ref):
    @pl.when(pl.program_id(2) == 0)
    def _():
        o_ref[...] = r_ref[...]

    o_ref[...] += jnp.dot(a_ref[...], w_ref[...], preferred_element_type=F32)


def matmul_ksplit(a, w, residual, *, tm, tn, tk, name):
    m, kdim = a.shape
    n = w.shape[1]
    tm, tn, tk = _tile(m, tm, 8), _tile(n, tn), _tile(kdim, tk)
    return pl.pallas_call(
        _matmul_ksplit_kernel,
        grid=(m // tm, n // tn, kdim // tk),
        in_specs=[pl.BlockSpec((tm, tk), lambda i, j, k: (i, k)),
                  pl.BlockSpec((tk, tn), lambda i, j, k: (k, j)),
                  pl.BlockSpec((tm, tn), lambda i, j, k: (i, j))],
        out_specs=pl.BlockSpec((tm, tn), lambda i, j, k: (i, j)),
        out_shape=jax.ShapeDtypeStruct((m, n), F32),
        compiler_params=_params("parallel", "parallel", "arbitrary"),
        name=name,
    )(a, w, residual)


SUBLANES = 8
CONV_ROWS = 64
CONV_HALO = 32
CONV_TS = 256
CONV_LANES = 256


def _sigmoid(x):
    return 0.5 * jnp.tanh(0.5 * x) + 0.5


def _conv_kernel(a_ref, gate_ref, w_ref, cb_ref, lng_ref, lnb_ref, o_ref, buf_ref, c_ref, *, ts, seq, cw, lc):
    i = pl.program_id(0)
    at_seq_start = (i * ts) % seq == 0

    @pl.when(at_seq_start)
    def _():
        buf_ref[0:CONV_HALO, :] = jnp.zeros((CONV_HALO, buf_ref.shape[1]), F32)

    @pl.when(jnp.logical_not(at_seq_start))
    def _():
        buf_ref[0:CONV_HALO, :] = buf_ref[ts:ts + CONV_HALO, :]

    buf_ref[CONV_HALO:CONV_HALO + ts, :] = a_ref[...] * _sigmoid(gate_ref[...])
    lead = CONV_HALO - (cw - 1)
    n_ch = a_ref.shape[1]

    def chunk(r, carry):
        r0 = pl.multiple_of(r * CONV_ROWS, CONV_ROWS)
        for c0 in range(0, n_ch, lc):
            cols = slice(c0, c0 + lc)
            acc = None
            for phase in range(SUBLANES):
                taps = [j for j in range(cw) if (lead + j) % SUBLANES == phase]
                if not taps:
                    continue
                rows = CONV_ROWS + (SUBLANES if phase else 0)
                part = None
                for j in taps:
                    base = (lead + j) // SUBLANES * SUBLANES
                    x = buf_ref[pl.ds(r0 + base, rows), cols].reshape(rows // SUBLANES, SUBLANES, lc)
                    term = x * w_ref[j, :, cols][None]
                    part = term if part is None else part + term
                part = part.reshape(rows, lc)
                piece = part[phase:phase + CONV_ROWS] if phase else part
                acc = piece if acc is None else acc + piece
            c_ref[pl.ds(r0, CONV_ROWS), cols] = acc + cb_ref[:, cols]
        c = c_ref[pl.ds(r0, CONV_ROWS), :]
        mu = jnp.mean(c, axis=-1, keepdims=True)
        cc = c - mu
        var = jnp.mean(cc * cc, axis=-1, keepdims=True)
        y = cc * lax.rsqrt(var + EPS) * lng_ref[...] + lnb_ref[...]
        o_ref[pl.ds(r0, CONV_ROWS), :] = (y * _sigmoid(y)).astype(o_ref.dtype)
        return carry

    lax.fori_loop(0, ts // CONV_ROWS, chunk, 0)


def conformer_conv(u, conv_w, conv_b, ln_g, ln_b, *, seq):
    m = u.shape[0]
    cw, c = conv_w.shape
    assert cw - 1 <= CONV_HALO
    ts = _tile(seq, CONV_TS, CONV_HALO)
    lc = _tile(c, CONV_LANES)
    row = lambda v: v.reshape(1, c)
    return pl.pallas_call(
        functools.partial(_conv_kernel, ts=ts, seq=seq, cw=cw, lc=lc),
        grid=(m // ts,),
        in_specs=[pl.BlockSpec((ts, c), lambda i: (i, 0)),
                  pl.BlockSpec((ts, c), lambda i: (i, 1)),
                  pl.BlockSpec((cw, SUBLANES, c), lambda i: (0, 0, 0)),
                  pl.BlockSpec((1, c), lambda i: (0, 0)),
                  pl.BlockSpec((1, c), lambda i: (0, 0)),
                  pl.BlockSpec((1, c), lambda i: (0, 0))],
        out_specs=pl.BlockSpec((ts, c), lambda i: (i, 0)),
        out_shape=jax.ShapeDtypeStruct((m, c), BF16),
        scratch_shapes=[pltpu.VMEM((CONV_HALO + ts, c), F32), pltpu.VMEM((ts, c), F32)],
        compiler_params=_params("arbitrary"),
        name="conformer_conv",
    )(u, u, jnp.broadcast_to(conv_w[:, None, :], (cw, SUBLANES, c)), row(conv_b), row(ln_g), row(ln_b))


def _mem_kv_kernel(mem_ref, gn_ref, w_ref, gk_ref, o_ref, *, n_k_heads):
    j = pl.program_id(1)
    hn = (_rms(mem_ref[...]) * gn_ref[0]).astype(BF16)
    kv = jnp.dot(hn, w_ref[0].astype(BF16), preferred_element_type=F32)
    kn = _rms(kv) * gk_ref[0]
    o_ref[0] = jnp.where(j < n_k_heads, kn, kv).astype(o_ref.dtype)


def mem_kv(mem2d, norm_mem, w_mem_kv, g_mem_k):
    nl, d, n2 = w_mem_kv.shape
    hd = g_mem_k.shape[-1]
    bm = mem2d.shape[0]
    return pl.pallas_call(
        functools.partial(_mem_kv_kernel, n_k_heads=n2 // 2 // hd),
        grid=(nl, n2 // hd),
        in_specs=[pl.BlockSpec((bm, d), lambda l, j: (0, 0)),
                  pl.BlockSpec((1, 1, d), lambda l, j: (l, 0, 0)),
                  pl.BlockSpec((1, d, hd), lambda l, j: (l, 0, j)),
                  pl.BlockSpec((1, 1, hd), lambda l, j: (l, 0, 0))],
        out_specs=pl.BlockSpec((1, bm, hd), lambda l, j: (l, 0, j)),
        out_shape=jax.ShapeDtypeStruct((nl, bm, n2), BF16),
        compiler_params=_params("parallel", "parallel"),
        name="mem_kv",
    )(mem2d, norm_mem.reshape(nl, 1, d), w_mem_kv, g_mem_k.reshape(nl, 1, hd))


def _mem_attn_kernel(q_ref, k_ref, v_ref, gq_ref, o_ref, *, n_heads, hd):
    scale = hd ** -0.5
    for h in range(n_heads):
        cols = slice(h * hd, (h + 1) * hd)
        q = (_rms(q_ref[:, cols]) * (gq_ref[...] * scale)).astype(BF16)
        s = lax.dot_general(q, k_ref[0, :, cols], (((1,), (1,)), ((), ())), preferred_element_type=F32)
        p = jnp.exp(s - jnp.max(s, axis=-1, keepdims=True))
        denom = jnp.sum(p, axis=-1, keepdims=True)
        o = jnp.dot(p.astype(BF16), v_ref[0, :, cols], preferred_element_type=F32)
        o_ref[:, cols] = (o / denom).astype(o_ref.dtype)


def mem_attention(qsrc, q_col_block, kv, layer, g_q, *, seq, mem_len):
    m = qsrc.shape[0]
    w = kv.shape[-1] // 2
    hd = g_q.shape[-1]
    ts = _tile(seq, 512, 8)
    tiles_per_seq = seq // ts
    return pl.pallas_call(
        functools.partial(_mem_attn_kernel, n_heads=w // hd, hd=hd),
        grid=(m // ts,),
        in_specs=[pl.BlockSpec((ts, w), lambda i: (i, q_col_block)),
                  pl.BlockSpec((1, mem_len, w), lambda i: (layer, i // tiles_per_seq, 0)),
                  pl.BlockSpec((1, mem_len, w), lambda i: (layer, i // tiles_per_seq, 1)),
                  pl.BlockSpec((1, hd), lambda i: (0, 0))],
        out_specs=pl.BlockSpec((ts, w), lambda i: (i, 0)),
        out_shape=jax.ShapeDtypeStruct((m, w), BF16),
        compiler_params=_params("parallel"),
        name="mem_attention",
    )(qsrc, kv, kv, g_q.reshape(1, hd))


def _rope(t, cos_ref, sin_ref):
    partner = pltpu.roll(t, LANES // 4, 1)
    return t * cos_ref[...] + partner * sin_ref[...]


def _dkv_kernel(h_ref, w_ref, ga_ref, gr_ref, cos_ref, sin_ref, ckv_ref, kr_ref, *, lora):
    ckr = jnp.dot(h_ref[...], w_ref[...], preferred_element_type=F32)
    ckv_ref[...] = (_rms(ckr[:, :lora]) * ga_ref[...]).astype(ckv_ref.dtype)
    kr = _rms(ckr[:, lora:]) * gr_ref[...]
    kr_ref[...] = _rope(kr, cos_ref, sin_ref).astype(kr_ref.dtype)


def kv_down(h, w_dkv_pad, g_a, g_kr_pad, cos_t, sin_t):
    m, d = h.shape
    lora = g_a.shape[-1]
    tm = _tile(m, 1024, 8)
    full = lambda shape: pl.BlockSpec(shape, lambda i: (0, 0))
    rows = lambda width: pl.BlockSpec((tm, width), lambda i: (i, 0))
    return pl.pallas_call(
        functools.partial(_dkv_kernel, lora=lora),
        grid=(m // tm,),
        in_specs=[rows(d), full(w_dkv_pad.shape), full((1, lora)), full((1, LANES)), rows(LANES), rows(LANES)],
        out_specs=[rows(lora), rows(LANES)],
        out_shape=[jax.ShapeDtypeStruct((m, lora), BF16), jax.ShapeDtypeStruct((m, LANES), BF16)],
        compiler_params=_params("parallel"),
        name="kv_down",
    )(h, w_dkv_pad, g_a.reshape(1, lora), g_kr_pad, cos_t, sin_t)


HEAD_PAIR = 2


def _ukv_kernel(c_ref, wk_ref, wv_ref, kr_ref, gk_ref, k_ref, v_ref, *, heads, nope):
    c = c_ref[...]
    kr = kr_ref[...]
    width = HEAD_PAIR * nope
    for pair in range(heads // HEAD_PAIR):
        kn = jnp.dot(c, wk_ref[:, pair * width:(pair + 1) * width], preferred_element_type=F32)
        for h in range(HEAD_PAIR):
            base = (pair * HEAD_PAIR + h) * 2 * nope
            k_ref[:, base:base + nope] = (_rms(kn[:, h * nope:(h + 1) * nope]) * gk_ref[...]).astype(k_ref.dtype)
            k_ref[:, base + nope:base + 2 * nope] = kr
        cols = slice(pair * width, (pair + 1) * width)
        v_ref[:, cols] = jnp.dot(c, wv_ref[:, cols], preferred_element_type=F32).astype(v_ref.dtype)


def kv_up(c_kv, w_kn, w_v, kr, g_kn, *, n_heads):
    m, lora = c_kv.shape
    nope = g_kn.shape[-1]
    dv = w_v.shape[1] // n_heads
    assert nope == LANES and dv == LANES and n_heads % HEAD_PAIR == 0
    tm = _tile(m, 512, 8)
    whole = lambda shape: pl.BlockSpec(shape, lambda i: (0, 0))
    rows = lambda width: pl.BlockSpec((tm, width), lambda i: (i, 0))
    return pl.pallas_call(
        functools.partial(_ukv_kernel, heads=n_heads, nope=nope),
        grid=(m // tm,),
        in_specs=[rows(lora), whole(w_kn.shape), whole(w_v.shape), rows(LANES), whole((1, nope))],
        out_specs=[rows(n_heads * 2 * nope), rows(n_heads * dv)],
        out_shape=[jax.ShapeDtypeStruct((m, n_heads * 2 * nope), BF16),
                   jax.ShapeDtypeStruct((m, n_heads * dv), BF16)],
        compiler_params=_params("parallel"),
        name="kv_up",
    )(c_kv, w_kn, w_v, kr, g_kn.reshape(1, nope))


def _q_kernel(u_ref, ga_ref, w_ref, gn_ref, gr_ref, cos_ref, sin_ref, q_ref, cq_ref, *, heads, nope, scale):
    cq_ref[...] = (_rms(u_ref[...]) * ga_ref[...]).astype(cq_ref.dtype)
    width = HEAD_PAIR * 2 * nope
    for pair in range(heads // HEAD_PAIR):
        q = jnp.dot(cq_ref[...], w_ref[:, pair * width:(pair + 1) * width], preferred_element_type=F32)
        for h in range(HEAD_PAIR):
            base = h * 2 * nope
            out = pair * width + base
            qn = _rms(q[:, base:base + nope]) * (gn_ref[...] * scale)
            q_ref[:, out:out + nope] = qn.astype(q_ref.dtype)
            qr = _rms(q[:, base + nope:base + 2 * nope]) * (gr_ref[...] * scale)
            q_ref[:, out + nope:out + 2 * nope] = _rope(qr, cos_ref, sin_ref).astype(q_ref.dtype)


def q_proj(u, g_qa, w_q_pad, g_qn, g_qr_pad, cos_t, sin_t, *, n_heads, scale):
    m = u.shape[0]
    lora = g_qa.shape[-1]
    nope = g_qn.shape[-1]
    assert nope == LANES and n_heads % HEAD_PAIR == 0
    tm = _tile(m, 512, 8)
    whole = lambda shape: pl.BlockSpec(shape, lambda i: (0, 0))
    rows = lambda width: pl.BlockSpec((tm, width), lambda i: (i, 0))
    return pl.pallas_call(
        functools.partial(_q_kernel, heads=n_heads, nope=nope, scale=scale),
        grid=(m // tm,),
        in_specs=[rows(lora), whole((1, lora)), whole(w_q_pad.shape), whole((1, nope)), whole((1, LANES)),
                  rows(LANES), rows(LANES)],
        out_specs=rows(n_heads * 2 * nope),
        out_shape=jax.ShapeDtypeStruct((m, n_heads * 2 * nope), BF16),
        scratch_shapes=[pltpu.VMEM((tm, lora), BF16)],
        compiler_params=_params("parallel"),
        name="q_proj",
    )(u, g_qa.reshape(1, lora), w_q_pad, g_qn.reshape(1, nope), g_qr_pad, cos_t, sin_t)


ATT_TQ = 2048
ATT_TK = 512
ATT_GROUP = 4


def _mla_kernel(q_ref, k_ref, v_ref, o_ref, m_ref, l_ref, acc_ref, *, tq, tk, group):
    i = pl.program_id(2)
    m_ref[...] = jnp.full(m_ref.shape, -jnp.inf, F32)
    l_ref[...] = jnp.zeros(l_ref.shape, F32)
    acc_ref[...] = jnp.zeros(acc_ref.shape, F32)
    chunks = tk // LANES

    def sub_block(start, diag_offset=None):
        rows = slice(0 if diag_offset is None else diag_offset, tq)
        s = lax.dot_general(q_ref[rows, :], k_ref[pl.ds(start, tk), :], (((1,), (1,)), ((), ())),
                            preferred_element_type=F32)
        if diag_offset is not None:
            row = lax.broadcasted_iota(jnp.int32, s.shape, 0)
            col = lax.broadcasted_iota(jnp.int32, s.shape, 1)
            s = jnp.where(col <= row, s, -jnp.inf)
        m_prev = m_ref[rows, :]
        m_new = jnp.maximum(m_prev, jnp.max(s, axis=-1, keepdims=True))
        alpha = jnp.exp2(m_prev - m_new)
        p = [jnp.exp2(s[:, c * LANES:(c + 1) * LANES] - m_new) for c in range(chunks)]
        l_ref[rows, :] = alpha * l_ref[rows, :] + functools.reduce(lambda a, b: a + b, p)
        pv = jnp.dot(jnp.concatenate(p, axis=1).astype(BF16), v_ref[pl.ds(start, tk), :],
                     preferred_element_type=F32)
        acc_ref[rows, :] = alpha * acc_ref[rows, :] + pv
        m_ref[rows, :] = m_new

    per_tile = tq // tk
    n_off = i * per_tile
    n_groups = n_off // group

    def body(g, carry):
        for t in range(group):
            sub_block(pl.multiple_of((g * group + t) * tk, tk))
        return carry

    lax.fori_loop(0, n_groups, body, 0)
    step = per_tile
    while group % step:
        step -= 1
    for rem in range(step, group, step):
        @pl.when(n_off % group == rem)
        def _(rem=rem):
            for t in range(rem):
                sub_block(pl.multiple_of((n_groups * group + t) * tk, tk))
    for t in range(per_tile):
        sub_block(pl.multiple_of(i * tq + t * tk, tk), diag_offset=t * tk)
    denom = jnp.sum(l_ref[...], axis=-1, keepdims=True)
    o_ref[...] = (acc_ref[...] / denom).astype(o_ref.dtype)


def mla_attention(q, k, v, *, batch, seq, n_heads):
    m = q.shape[0]
    dqk = q.shape[1] // n_heads
    dv = v.shape[1] // n_heads
    assert dv == LANES
    tq = _tile(seq, ATT_TQ, ATT_TK)
    tk = min(ATT_TK, tq)
    nq = seq // tq
    return pl.pallas_call(
        functools.partial(_mla_kernel, tq=tq, tk=tk, group=ATT_GROUP),
        grid=(batch, n_heads, nq),
        in_specs=[pl.BlockSpec((tq, dqk), lambda b, h, i: (b * nq + i, h)),
                  pl.BlockSpec((seq, dqk), lambda b, h, i: (b, h)),
                  pl.BlockSpec((seq, dv), lambda b, h, i: (b, h))],
        out_specs=pl.BlockSpec((tq, dv), lambda b, h, i: (b * nq + i, h)),
        out_shape=jax.ShapeDtypeStruct((m, n_heads * dv), BF16),
        scratch_shapes=[pltpu.VMEM((tq, LANES), F32), pltpu.VMEM((tq, LANES), F32), pltpu.VMEM((tq, dv), F32)],
        compiler_params=_params("parallel", "parallel", "arbitrary"),
        name="mla_attention",
    )(q, k, v)


WIDE_TM, WIDE_TN = 2048, 512


def _wide(a_parts, w_f32, **kw):
    return matmul(a_parts, w_f32, tm=WIDE_TM, tn=WIDE_TN, single_buffer_a=True, **kw)


def _mlp(x, gain, w_in, w_out, layer):
    h, = rmsnorm_cast(x, gain.reshape(1, -1))
    kw = dict(layer=layer, out_dtype=BF16, act="relu2", name="mlp_in")
    if side_cast_fits(w_out.shape[1], x.shape[0], w_in.shape[-1], WIDE_TM, WIDE_TN):
        hm, w_out_bf16 = _wide([h], w_in, side_cast=(w_out, layer), **kw)
    else:
        hm, w_out_bf16 = _wide([h], w_in, **kw), w_out[layer].astype(BF16)
    return matmul_ksplit(hm, w_out_bf16, x, tm=1024, tn=1024, tk=4096, name="mlp_out")


def kernel(x, mem, positions, norm_mix, norm_mlp, norm_mem, w_mem_kv, g_mem_q, g_mem_k, w_mlp_in, w_mlp_out,
           a_w_in, a_conv_w, a_conv_b, a_ln_g, a_ln_b, a_w_out, b_w_in, b_g_qa, b_w_uq, b_g_qn, b_g_qr, b_w_out,
           kv_g_in, kv_w_dkv, kv_g_a, kv_w_ukv, kv_g_kn, kv_g_kr):
    batch, seq, d = x.shape
    m = batch * seq
    mem_len = mem.shape[1]
    mem_w = w_mem_kv.shape[-1] // 2
    conv_ch = a_conv_w.shape[-1]
    q_lora = b_g_qa.shape[-1]
    kv_lora = kv_g_a.shape[-1]
    nope, rope = b_g_qn.shape[-1], b_g_qr.shape[-1]
    n_heads = b_w_uq.shape[-1] // (nope + rope)
    dv = kv_w_ukv.shape[-1] // n_heads - nope
    assert 2 * rope == LANES and nope == LANES and dv == LANES
    assert (2 * conv_ch) % mem_w == 0 and q_lora % mem_w == 0
    assert b_w_in.shape[0] == 1 and a_w_in.shape[0] == 1 and norm_mix.shape[0] == 2

    cast = lambda w: w.astype(BF16)
    w_uq = b_w_uq[0].reshape(q_lora, n_heads, nope + rope)
    w_q_pad = cast(jnp.concatenate([w_uq, w_uq[:, :, nope:]], axis=-1).reshape(q_lora, n_heads * 2 * nope))
    w_ukv = kv_w_ukv.reshape(kv_lora, n_heads, nope + dv)
    w_kn = cast(w_ukv[:, :, :nope].reshape(kv_lora, n_heads * nope))
    w_v = cast(w_ukv[:, :, nope:].reshape(kv_lora, n_heads * dv))
    w_dkv_pad = cast(jnp.concatenate([kv_w_dkv, kv_w_dkv[:, kv_lora:]], axis=-1))
    pad_gain = lambda g: jnp.concatenate([g, g]).reshape(1, 2 * rope)

    inv_freq = ROPE_THETA ** (-jnp.arange(0, rope, 2, dtype=F32) / rope)
    ang = positions.astype(F32).reshape(m, 1) * inv_freq
    zeros = jnp.zeros((m, LANES - rope), F32)
    cos_t = jnp.concatenate([jnp.cos(ang), jnp.cos(ang), zeros], axis=-1)
    sin_t = jnp.concatenate([-jnp.sin(ang), jnp.sin(ang), zeros], axis=-1)

    x = x.reshape(m, d)
    kv_mem = mem_kv(mem.reshape(batch * mem_len, d), norm_mem, w_mem_kv, g_mem_k)

    out_proj = functools.partial(matmul, out_dtype=F32, tm=1024, tn=512)
    h, = rmsnorm_cast(x, norm_mix[0:1])
    u = _wide([h], a_w_in[0], out_dtype=F32, name="a_in_proj")
    y_main = conformer_conv(u, a_conv_w[0], a_conv_b[0], a_ln_g[0], a_ln_b[0], seq=seq)
    y_mem = mem_attention(u, 2 * conv_ch // mem_w, kv_mem, 0, g_mem_q[0], seq=seq, mem_len=mem_len)
    x = out_proj([y_main, y_mem], cast(a_w_out[0]), residual=x, name="a_out_proj")
    x = _mlp(x, norm_mlp[0], w_mlp_in, w_mlp_out, 0)

    h_kv, h = rmsnorm_cast(x, jnp.stack([kv_g_in, norm_mix[1]]))
    c_kv, k_rope = kv_down(h_kv, w_dkv_pad, kv_g_a, pad_gain(kv_g_kr), cos_t, sin_t)
    k_all, v_all = kv_up(c_kv, w_kn, w_v, k_rope, kv_g_kn, n_heads=n_heads)
    u = matmul([h], cast(b_w_in[0]), out_dtype=F32, tm=1024, tn=1024, name="b_in_proj")
    q_all = q_proj(u, b_g_qa[0], w_q_pad, b_g_qn[0], pad_gain(b_g_qr[0]), cos_t, sin_t,
                   n_heads=n_heads, scale=(nope + rope) ** -0.5 * math.log2(math.e))
    y_main = mla_attention(q_all, k_all, v_all, batch=batch, seq=seq, n_heads=n_heads)
    y_mem = mem_attention(u, q_lora // mem_w, kv_mem, 1, g_mem_q[1], seq=seq, mem_len=mem_len)
    x = out_proj([y_main, y_mem], cast(b_w_out[0]), residual=x, name="b_out_proj")
    x = _mlp(x, norm_mlp[1], w_mlp_in, w_mlp_out, 1)
    return x.reshape(batch, seq, d)
```

```python
import functools
import math

import jax
import jax.numpy as jnp
from jax import lax
from jax.experimental import pallas as pl
from jax.experimental.pallas import tpu as pltpu

EPS = 1e-6
ROPE_THETA = 10000.0
LANES = 128
VMEM_LIMIT_BYTES = 56 << 20
KSPLIT_STATS_VMEM_BYTES = 62 << 20
F32 = jnp.float32
BF16 = jnp.bfloat16


def _tile(n, pref, mult=LANES):
    if n <= pref:
        return n
    t = (pref // mult) * mult
    while t >= mult:
        if n % t == 0:
            return t
        t -= mult
    raise ValueError(f"no tile for {n} (pref {pref}, mult {mult})")


def _params(*semantics, vmem_limit_bytes=VMEM_LIMIT_BYTES):
    return pltpu.CompilerParams(dimension_semantics=semantics, vmem_limit_bytes=vmem_limit_bytes)


def _rms(x, width=None):
    width = x.shape[-1] if width is None else width
    ms = jnp.sum(x * x, axis=-1, keepdims=True) * (1.0 / width)
    return x * lax.rsqrt(ms + EPS)


def _rmsnorm_kernel(x_ref, g_ref, *o_refs):
    xn = _rms(x_ref[...])
    for i, o_ref in enumerate(o_refs):
        o_ref[...] = (xn * g_ref[i:i + 1, :]).astype(o_ref.dtype)


def rmsnorm_cast(x, gains):
    m, d = x.shape
    n = gains.shape[0]
    tm = _tile(m, 512, 8)
    return pl.pallas_call(
        _rmsnorm_kernel,
        grid=(m // tm,),
        in_specs=[pl.BlockSpec((tm, d), lambda i: (i, 0)),
                  pl.BlockSpec((n, d), lambda i: (0, 0))],
        out_specs=[pl.BlockSpec((tm, d), lambda i: (i, 0)) for _ in range(n)],
        out_shape=[jax.ShapeDtypeStruct((m, d), BF16) for _ in range(n)],
        compiler_params=_params("parallel"),
        name="rmsnorm_cast",
    )(x, gains)


def _row_stats(x, xb_ref, ssq_ref, first_col_tile):
    xb_ref[...] = x.astype(xb_ref.dtype)
    sq = x * x
    partial = functools.reduce(lambda p, q: p + q,
                               [sq[:, c * LANES:(c + 1) * LANES] for c in range(x.shape[1] // LANES)])

    @pl.when(first_col_tile)
    def _():
        ssq_ref[...] = partial

    @pl.when(jnp.logical_not(first_col_tile))
    def _():
        ssq_ref[...] += partial


def _matmul_kernel(*refs, names, act, norm_dim):
    r = dict(zip(names, refs))
    if "side_in" in r:
        r["side_out"][...] = r["side_in"][...].astype(BF16)
    acc = None
    for key in names:
        if not key.startswith("a"):
            continue
        w = r["w" + key[1:]][...]
        if "gain" in r:
            g = r["gain"][...]
            w = jnp.concatenate([w[:, c * LANES:(c + 1) * LANES] * g for c in range(w.shape[1] // LANES)], axis=1)
        part = jnp.dot(r[key][...], w.astype(BF16), preferred_element_type=F32)
        acc = part if acc is None else acc + part
    if "ssq" in r:
        acc = acc * lax.rsqrt(jnp.sum(r["ssq"][...], axis=-1, keepdims=True) * (1.0 / norm_dim) + EPS)
    if act == "relu2":
        acc = jnp.square(jnp.maximum(acc, 0.0))
    if "res" in r:
        acc = acc + r["res"][...]
    r["out"][...] = acc.astype(r["out"].dtype)
    if "xb" in r:
        _row_stats(acc, r["xb"], r["ssq_out"], pl.program_id(1) == 0)


def matmul(a_parts, w, *, out_dtype, act=None, residual=None, tm, tn, single_buffer_a=False, layer=None,
           side_cast=None, norm=None, emit_stats=False, name):
    m = a_parts[0].shape[0]
    n = w.shape[-1]
    tm, tn = _tile(m, tm, 8), _tile(n, tn)
    nj = n // tn
    a_mode = dict(pipeline_mode=pl.Buffered(1)) if single_buffer_a else {}
    names, in_specs, args = [], [], []

    def add(name_, spec, arr):
        names.append(name_)
        in_specs.append(spec)
        args.append(arr)

    offset = 0
    for p, a in enumerate(a_parts):
        kp = a.shape[1]
        assert offset % kp == 0
        add(f"a{p}", pl.BlockSpec((tm, kp), lambda i, j: (i, 0), **a_mode), a)
        if layer is None:
            add(f"w{p}", pl.BlockSpec((kp, tn), lambda i, j, blk=offset // kp: (blk, j)), w)
        else:
            add(f"w{p}", pl.BlockSpec((None, kp, tn), lambda i, j, blk=offset // kp: (layer, blk, j)), w)
        offset += kp
    kdim = offset
    assert kdim == w.shape[-2]
    if norm is not None:
        gain, ssq = norm
        assert len(a_parts) == 1 and w.dtype == F32
        add("gain", pl.BlockSpec((kdim, LANES), lambda i, j: (0, 0)),
            jnp.broadcast_to(gain.reshape(kdim, 1), (kdim, LANES)))
        add("ssq", pl.BlockSpec((tm, LANES), lambda i, j: (i, 0)), ssq)
    if residual is not None:
        add("res", pl.BlockSpec((tm, tn), lambda i, j: (i, j)), residual)
    if side_cast is not None:
        src, src_layer = side_cast
        slab = src.shape[1] // ((m // tm) * nj)
        add("side_in", pl.BlockSpec((None, slab, src.shape[2]), lambda i, j: (src_layer, i * nj + j, 0)), src)
    names.append("out")
    out_specs = [pl.BlockSpec((tm, tn), lambda i, j: (i, j))]
    out_shape = [jax.ShapeDtypeStruct((m, n), out_dtype)]
    if side_cast is not None:
        names.append("side_out")
        out_specs.append(pl.BlockSpec((slab, src.shape[2]), lambda i, j: (i * nj + j, 0)))
        out_shape.append(jax.ShapeDtypeStruct(src.shape[1:], BF16))
    if emit_stats:
        assert out_dtype == F32
        names += ["xb", "ssq_out"]
        out_specs += [pl.BlockSpec((tm, tn), lambda i, j: (i, j)), pl.BlockSpec((tm, LANES), lambda i, j: (i, 0))]
        out_shape += [jax.ShapeDtypeStruct((m, n), BF16), jax.ShapeDtypeStruct((m, LANES), F32)]
    outs = pl.pallas_call(
        functools.partial(_matmul_kernel, names=tuple(names), act=act, norm_dim=kdim),
        grid=(m // tm, nj),
        in_specs=in_specs,
        out_specs=out_specs,
        out_shape=out_shape,
        compiler_params=_params("parallel", "arbitrary"),
        name=name,
    )(*args)
    return outs if len(outs) > 1 else outs[0]


def side_cast_fits(rows, m, n, tm, tn):
    steps = (m // _tile(m, tm, 8)) * (n // _tile(n, tn))
    return rows % steps == 0 and (rows // steps) % 16 == 0


def _matmul_ksplit_kernel(a_ref, w_ref, r_ref, o_ref, *stats_refs):
    k = pl.program_id(2)

    @pl.when(k == 0)
    def _():
        o_ref[...] = r_ref[...]

    o_ref[...] += jnp.dot(a_ref[...], w_ref[...], preferred_element_type=F32)
    if stats_refs:
        @pl.when(k == pl.num_programs(2) - 1)
        def _():
            _row_stats(o_ref[...], *stats_refs, pl.program_id(1) == 0)


def matmul_ksplit(a, w, residual, *, tm, tn, tk, emit_stats=False, name):
    m, kdim = a.shape
    n = w.shape[1]
    tm, tn, tk = _tile(m, tm, 8), _tile(n, tn), _tile(kdim, tk)
    nj = n // tn
    out_specs = [pl.BlockSpec((tm, tn), lambda i, j, k: (i, j))]
    out_shape = [jax.ShapeDtypeStruct((m, n), F32)]
    if emit_stats:
        out_specs += [pl.BlockSpec((tm, tn), lambda i, j, k: (i, j)),
                      pl.BlockSpec((tm, LANES), lambda i, j, k: (i, 0))]
        out_shape += [jax.ShapeDtypeStruct((m, n), BF16), jax.ShapeDtypeStruct((m, LANES), F32)]
    outs = pl.pallas_call(
        _matmul_ksplit_kernel,
        grid=(m // tm, nj, kdim // tk),
        in_specs=[pl.BlockSpec((tm, tk), lambda i, j, k: (i, k)),
                  pl.BlockSpec((tk, tn), lambda i, j, k: (k, j)),
                  pl.BlockSpec((tm, tn), lambda i, j, k: (i, j))],
        out_specs=out_specs,
        out_shape=out_shape,
        compiler_params=_params("parallel", "arbitrary", "arbitrary",
                                vmem_limit_bytes=KSPLIT_STATS_VMEM_BYTES if emit_stats else VMEM_LIMIT_BYTES),
        name=name,
    )(a, w, residual)
    return outs if emit_stats else outs[0]


SUBLANES = 8
CONV_ROWS = 64
CONV_HALO = 32
CONV_TS = 256
CONV_LANES = 256


def _sigmoid(x):
    return 0.5 * jnp.tanh(0.5 * x) + 0.5


def _conv_kernel(a_ref, gate_ref, w_ref, cb_ref, lng_ref, lnb_ref, o_ref, buf_ref, c_ref, *, ts, seq, cw, lc):
    i = pl.program_id(0)
    at_seq_start = (i * ts) % seq == 0

    @pl.when(at_seq_start)
    def _():
        buf_ref[0:CONV_HALO, :] = jnp.zeros((CONV_HALO, buf_ref.shape[1]), F32)

    @pl.when(jnp.logical_not(at_seq_start))
    def _():
        buf_ref[0:CONV_HALO, :] = buf_ref[ts:ts + CONV_HALO, :]

    buf_ref[CONV_HALO:CONV_HALO + ts, :] = a_ref[...] * _sigmoid(gate_ref[...])
    lead = CONV_HALO - (cw - 1)
    n_ch = a_ref.shape[1]

    def chunk(r, carry):
        r0 = pl.multiple_of(r * CONV_ROWS, CONV_ROWS)
        for c0 in range(0, n_ch, lc):
            cols = slice(c0, c0 + lc)
            acc = None
            for phase in range(SUBLANES):
                taps = [j for j in range(cw) if (lead + j) % SUBLANES == phase]
                if not taps:
                    continue
                rows = CONV_ROWS + (SUBLANES if phase else 0)
                part = None
                for j in taps:
                    base = (lead + j) // SUBLANES * SUBLANES
                    x = buf_ref[pl.ds(r0 + base, rows), cols].reshape(rows // SUBLANES, SUBLANES, lc)
                    term = x * w_ref[j, :, cols][None]
                    part = term if part is None else part + term
                part = part.reshape(rows, lc)
                piece = part[phase:phase + CONV_ROWS] if phase else part
                acc = piece if acc is None else acc + piece
            c_ref[pl.ds(r0, CONV_ROWS), cols] = acc + cb_ref[:, cols]
        c = c_ref[pl.ds(r0, CONV_ROWS), :]
        mu = jnp.mean(c, axis=-1, keepdims=True)
        cc = c - mu
        var = jnp.mean(cc * cc, axis=-1, keepdims=True)
        y = cc * lax.rsqrt(var + EPS) * lng_ref[...] + lnb_ref[...]
        o_ref[pl.ds(r0, CONV_ROWS), :] = (y * _sigmoid(y)).astype(o_ref.dtype)
        return carry

    lax.fori_loop(0, ts // CONV_ROWS, chunk, 0)


def conformer_conv(u, conv_w, conv_b, ln_g, ln_b, *, seq):
    m = u.shape[0]
    cw, c = conv_w.shape
    assert cw - 1 <= CONV_HALO
    ts = _tile(seq, CONV_TS, CONV_HALO)
    lc = _tile(c, CONV_LANES)
    row = lambda v: v.reshape(1, c)
    return pl.pallas_call(
        functools.partial(_conv_kernel, ts=ts, seq=seq, cw=cw, lc=lc),
        grid=(m // ts,),
        in_specs=[pl.BlockSpec((ts, c), lambda i: (i, 0)),
                  pl.BlockSpec((ts, c), lambda i: (i, 1)),
                  pl.BlockSpec((cw, SUBLANES, c), lambda i: (0, 0, 0)),
                  pl.BlockSpec((1, c), lambda i: (0, 0)),
                  pl.BlockSpec((1, c), lambda i: (0, 0)),
                  pl.BlockSpec((1, c), lambda i: (0, 0))],
        out_specs=pl.BlockSpec((ts, c), lambda i: (i, 0)),
        out_shape=jax.ShapeDtypeStruct((m, c), BF16),
        scratch_shapes=[pltpu.VMEM((CONV_HALO + ts, c), F32), pltpu.VMEM((ts, c), F32)],
        compiler_params=_params("arbitrary"),
        name="conformer_conv",
    )(u, u, jnp.broadcast_to(conv_w[:, None, :], (cw, SUBLANES, c)), row(conv_b), row(ln_g), row(ln_b))


def _mem_kv_kernel(mem_ref, gn_ref, w_ref, gk_ref, o_ref, *, n_k_heads):
    j = pl.program_id(1)
    hn = (_rms(mem_ref[...]) * gn_ref[0]).astype(BF16)
    kv = jnp.dot(hn, w_ref[0].astype(BF16), preferred_element_type=F32)
    kn = _rms(kv) * gk_ref[0]
    o_ref[0] = jnp.where(j < n_k_heads, kn, kv).astype(o_ref.dtype)


def mem_kv(mem2d, norm_mem, w_mem_kv, g_mem_k):
    nl, d, n2 = w_mem_kv.shape
    hd = g_mem_k.shape[-1]
    bm = mem2d.shape[0]
    return pl.pallas_call(
        functools.partial(_mem_kv_kernel, n_k_heads=n2 // 2 // hd),
        grid=(nl, n2 // hd),
        in_specs=[pl.BlockSpec((bm, d), lambda l, j: (0, 0)),
                  pl.BlockSpec((1, 1, d), lambda l, j: (l, 0, 0)),
                  pl.BlockSpec((1, d, hd), lambda l, j: (l, 0, j)),
                  pl.BlockSpec((1, 1, hd), lambda l, j: (l, 0, 0))],
        out_specs=pl.BlockSpec((1, bm, hd), lambda l, j: (l, 0, j)),
        out_shape=jax.ShapeDtypeStruct((nl, bm, n2), BF16),
        compiler_params=_params("parallel", "parallel"),
        name="mem_kv",
    )(mem2d, norm_mem.reshape(nl, 1, d), w_mem_kv, g_mem_k.reshape(nl, 1, hd))


def _mem_attn_kernel(q_ref, k_ref, v_ref, gq_ref, o_ref, *, n_heads, hd):
    scale = hd ** -0.5
    for h in range(n_heads):
        cols = slice(h * hd, (h + 1) * hd)
        q = (_rms(q_ref[:, cols]) * (gq_ref[...] * scale)).astype(BF16)
        s = lax.dot_general(q, k_ref[0, :, cols], (((1,), (1,)), ((), ())), preferred_element_type=F32)
        p = jnp.exp(s - jnp.max(s, axis=-1, keepdims=True))
        denom = jnp.sum(p, axis=-1, keepdims=True)
        o = jnp.dot(p.astype(BF16), v_ref[0, :, cols], preferred_element_type=F32)
        o_ref[:, cols] = (o / denom).astype(o_ref.dtype)


def mem_attention(qsrc, q_col_block, kv, layer, g_q, *, seq, mem_len):
    m = qsrc.shape[0]
    w = kv.shape[-1] // 2
    hd = g_q.shape[-1]
    ts = _tile(seq, 512, 8)
    tiles_per_seq = seq // ts
    return pl.pallas_call(
        functools.partial(_mem_attn_kernel, n_heads=w // hd, hd=hd),
        grid=(m // ts,),
        in_specs=[pl.BlockSpec((ts, w), lambda i: (i, q_col_block)),
                  pl.BlockSpec((1, mem_len, w), lambda i: (layer, i // tiles_per_seq, 0)),
                  pl.BlockSpec((1, mem_len, w), lambda i: (layer, i // tiles_per_seq, 1)),
                  pl.BlockSpec((1, hd), lambda i: (0, 0))],
        out_specs=pl.BlockSpec((ts, w), lambda i: (i, 0)),
        out_shape=jax.ShapeDtypeStruct((m, w), BF16),
        compiler_params=_params("parallel"),
        name="mem_attention",
    )(qsrc, kv, kv, g_q.reshape(1, hd))


def _rope(t, cos_ref, sin_ref):
    partner = pltpu.roll(t, LANES // 4, 1)
    return t * cos_ref[...] + partner * sin_ref[...]


def _dkv_kernel(xb_ref, ssq_ref, gain_ref, w_ref, ga_ref, gr_ref, cos_ref, sin_ref, ckv_ref, kr_ref, *, lora, dim):
    g = gain_ref[...]
    w = w_ref[...]
    w = jnp.concatenate([w[:, c * LANES:(c + 1) * LANES] * g for c in range(w.shape[1] // LANES)], axis=1)
    ckr = jnp.dot(xb_ref[...], w.astype(BF16), preferred_element_type=F32)
    ckr = ckr * lax.rsqrt(jnp.sum(ssq_ref[...], axis=-1, keepdims=True) * (1.0 / dim) + EPS)
    ckv_ref[...] = (_rms(ckr[:, :lora]) * ga_ref[...]).astype(ckv_ref.dtype)
    kr = _rms(ckr[:, lora:]) * gr_ref[...]
    kr_ref[...] = _rope(kr, cos_ref, sin_ref).astype(kr_ref.dtype)


def kv_down(xb, ssq, gain, w_dkv_pad, g_a, g_kr_pad, cos_t, sin_t):
    m, d = xb.shape
    lora = g_a.shape[-1]
    tm = _tile(m, 1024, 8)
    full = lambda shape: pl.BlockSpec(shape, lambda i: (0, 0))
    rows = lambda width: pl.BlockSpec((tm, width), lambda i: (i, 0))
    return pl.pallas_call(
        functools.partial(_dkv_kernel, lora=lora, dim=d),
        grid=(m // tm,),
        in_specs=[rows(d), rows(LANES), full((d, LANES)), full(w_dkv_pad.shape), full((1, lora)), full((1, LANES)),
                  rows(LANES), rows(LANES)],
        out_specs=[rows(lora), rows(LANES)],
        out_shape=[jax.ShapeDtypeStruct((m, lora), BF16), jax.ShapeDtypeStruct((m, LANES), BF16)],
        compiler_params=_params("parallel"),
        name="kv_down",
    )(xb, ssq, jnp.broadcast_to(gain.reshape(d, 1), (d, LANES)), w_dkv_pad, g_a.reshape(1, lora), g_kr_pad,
      cos_t, sin_t)


HEAD_PAIR = 2


def _ukv_kernel(c_ref, wk_ref, wv_ref, kr_ref, gk_ref, k_ref, v_ref, *, heads, nope):
    c = c_ref[...]
    kr = kr_ref[...]
    width = HEAD_PAIR * nope
    for pair in range(heads // HEAD_PAIR):
        kn = jnp.dot(c, wk_ref[:, pair * width:(pair + 1) * width], preferred_element_type=F32)
        for h in range(HEAD_PAIR):
            base = (pair * HEAD_PAIR + h) * 2 * nope
            k_ref[:, base:base + nope] = (_rms(kn[:, h * nope:(h + 1) * nope]) * gk_ref[...]).astype(k_ref.dtype)
            k_ref[:, base + nope:base + 2 * nope] = kr
        cols = slice(pair * width, (pair + 1) * width)
        v_ref[:, cols] = jnp.dot(c, wv_ref[:, cols], preferred_element_type=F32).astype(v_ref.dtype)


def kv_up(c_kv, w_kn, w_v, kr, g_kn, *, n_heads):
    m, lora = c_kv.shape
    nope = g_kn.shape[-1]
    dv = w_v.shape[1] // n_heads
    assert nope == LANES and dv == LANES and n_heads % HEAD_PAIR == 0
    tm = _tile(m, 512, 8)
    whole = lambda shape: pl.BlockSpec(shape, lambda i: (0, 0))
    rows = lambda width: pl.BlockSpec((tm, width), lambda i: (i, 0))
    return pl.pallas_call(
        functools.partial(_ukv_kernel, heads=n_heads, nope=nope),
        grid=(m // tm,),
        in_specs=[rows(lora), whole(w_kn.shape), whole(w_v.shape), rows(LANES), whole((1, nope))],
        out_specs=[rows(n_heads * 2 * nope), rows(n_heads * dv)],
        out_shape=[jax.ShapeDtypeStruct((m, n_heads * 2 * nope), BF16),
                   jax.ShapeDtypeStruct((m, n_heads * dv), BF16)],
        compiler_params=_params("parallel"),
        name="kv_up",
    )(c_kv, w_kn, w_v, kr, g_kn.reshape(1, nope))


def _q_kernel(u_ref, ga_ref, w_ref, gn_ref, gr_ref, cos_ref, sin_ref, q_ref, cq_ref, *, heads, nope, scale):
    cq_ref[...] = (_rms(u_ref[...]) * ga_ref[...]).astype(cq_ref.dtype)
    width = HEAD_PAIR * 2 * nope
    for pair in range(heads // HEAD_PAIR):
        q = jnp.dot(cq_ref[...], w_ref[:, pair * width:(pair + 1) * width], preferred_element_type=F32)
        for h in range(HEAD_PAIR):
            base = h * 2 * nope
            out = pair * width + base
            qn = _rms(q[:, base:base + nope]) * (gn_ref[...] * scale)
            q_ref[:, out:out + nope] = qn.astype(q_ref.dtype)
            qr = _rms(q[:, base + nope:base + 2 * nope]) * (gr_ref[...] * scale)
            q_ref[:, out + nope:out + 2 * nope] = _rope(qr, cos_ref, sin_ref).astype(q_ref.dtype)


def q_proj(u, g_qa, w_q_pad, g_qn, g_qr_pad, cos_t, sin_t, *, n_heads, scale):
    m = u.shape[0]
    lora = g_qa.shape[-1]
    nope = g_qn.shape[-1]
    assert nope == LANES and n_heads % HEAD_PAIR == 0
    tm = _tile(m, 512, 8)
    whole = lambda shape: pl.BlockSpec(shape, lambda i: (0, 0))
    rows = lambda width: pl.BlockSpec((tm, width), lambda i: (i, 0))
    return pl.pallas_call(
        functools.partial(_q_kernel, heads=n_heads, nope=nope, scale=scale),
        grid=(m // tm,),
        in_specs=[rows(lora), whole((1, lora)), whole(w_q_pad.shape), whole((1, nope)), whole((1, LANES)),
                  rows(LANES), rows(LANES)],
        out_specs=rows(n_heads * 2 * nope),
        out_shape=jax.ShapeDtypeStruct((m, n_heads * 2 * nope), BF16),
        scratch_shapes=[pltpu.VMEM((tm, lora), BF16)],
        compiler_params=_params("parallel"),
        name="q_proj",
    )(u, g_qa.reshape(1, lora), w_q_pad, g_qn.reshape(1, nope), g_qr_pad, cos_t, sin_t)


ATT_TQ = 2048
ATT_TK = 512
ATT_GROUP = 4


def _mla_kernel(q_ref, k_ref, v_ref, o_ref, m_ref, l_ref, acc_ref, *, tq, tk, group):
    i = pl.program_id(2)
    m_ref[...] = jnp.full(m_ref.shape, -jnp.inf, F32)
    l_ref[...] = jnp.zeros(l_ref.shape, F32)
    acc_ref[...] = jnp.zeros(acc_ref.shape, F32)
    chunks = tk // LANES

    def sub_block(start, diag_offset=None):
        rows = slice(0 if diag_offset is None else diag_offset, tq)
        s = lax.dot_general(q_ref[rows, :], k_ref[pl.ds(start, tk), :], (((1,), (1,)), ((), ())),
                            preferred_element_type=F32)
        if diag_offset is not None:
            row = lax.broadcasted_iota(jnp.int32, s.shape, 0)
            col = lax.broadcasted_iota(jnp.int32, s.shape, 1)
            s = jnp.where(col <= row, s, -jnp.inf)
        m_prev = m_ref[rows, :]
        m_new = jnp.maximum(m_prev, jnp.max(s, axis=-1, keepdims=True))
        alpha = jnp.exp2(m_prev - m_new)
        p = [jnp.exp2(s[:, c * LANES:(c + 1) * LANES] - m_new) for c in range(chunks)]
        l_ref[rows, :] = alpha * l_ref[rows, :] + functools.reduce(lambda a, b: a + b, p)
        pv = jnp.dot(jnp.concatenate(p, axis=1).astype(BF16), v_ref[pl.ds(start, tk), :],
                     preferred_element_type=F32)
        acc_ref[rows, :] = alpha * acc_ref[rows, :] + pv
        m_ref[rows, :] = m_new

    per_tile = tq // tk
    n_off = i * per_tile
    n_groups = n_off // group

    def body(g, carry):
        for t in range(group):
            sub_block(pl.multiple_of((g * group + t) * tk, tk))
        return carry

    lax.fori_loop(0, n_groups, body, 0)
    step = per_tile
    while group % step:
        step -= 1
    for rem in range(step, group, step):
        @pl.when(n_off % group == rem)
        def _(rem=rem):
            for t in range(rem):
                sub_block(pl.multiple_of((n_groups * group + t) * tk, tk))
    for t in range(per_tile):
        sub_block(pl.multiple_of(i * tq + t * tk, tk), diag_offset=t * tk)
    denom = jnp.sum(l_ref[...], axis=-1, keepdims=True)
    o_ref[...] = (acc_ref[...] / denom).astype(o_ref.dtype)


def mla_attention(q, k, v, *, batch, seq, n_heads):
    m = q.shape[0]
    dqk = q.shape[1] // n_heads
    dv = v.shape[1] // n_heads
    assert dv == LANES
    tq = _tile(seq, ATT_TQ, ATT_TK)
    tk = min(ATT_TK, tq)
    nq = seq // tq
    return pl.pallas_call(
        functools.partial(_mla_kernel, tq=tq, tk=tk, group=ATT_GROUP),
        grid=(batch, n_heads, nq),
        in_specs=[pl.BlockSpec((tq, dqk), lambda b, h, i: (b * nq + i, h)),
                  pl.BlockSpec((seq, dqk), lambda b, h, i: (b, h)),
                  pl.BlockSpec((seq, dv), lambda b, h, i: (b, h))],
        out_specs=pl.BlockSpec((tq, dv), lambda b, h, i: (b * nq + i, h)),
        out_shape=jax.ShapeDtypeStruct((m, n_heads * dv), BF16),
        scratch_shapes=[pltpu.VMEM((tq, LANES), F32), pltpu.VMEM((tq, LANES), F32), pltpu.VMEM((tq, dv), F32)],
        compiler_params=_params("parallel", "parallel", "arbitrary"),
        name="mla_attention",
    )(q, k, v)


WIDE_TM, WIDE_TN = 2048, 512


def _wide(a_parts, w_f32, **kw):
    return matmul(a_parts, w_f32, tm=WIDE_TM, tn=WIDE_TN, single_buffer_a=True, **kw)


def _mlp(x, xb, ssq, gain, w_in, w_out, layer, emit_stats):
    kw = dict(layer=layer, norm=(gain, ssq), out_dtype=BF16, act="relu2", name="mlp_in")
    if side_cast_fits(w_out.shape[1], x.shape[0], w_in.shape[-1], WIDE_TM, WIDE_TN):
        hm, w_out_bf16 = _wide([xb], w_in, side_cast=(w_out, layer), **kw)
    else:
        hm, w_out_bf16 = _wide([xb], w_in, **kw), w_out[layer].astype(BF16)
    return matmul_ksplit(hm, w_out_bf16, x, tm=1024, tn=1024, tk=4096, emit_stats=emit_stats, name="mlp_out")


def kernel(x, mem, positions, norm_mix, norm_mlp, norm_mem, w_mem_kv, g_mem_q, g_mem_k, w_mlp_in, w_mlp_out,
           a_w_in, a_conv_w, a_conv_b, a_ln_g, a_ln_b, a_w_out, b_w_in, b_g_qa, b_w_uq, b_g_qn, b_g_qr, b_w_out,
           kv_g_in, kv_w_dkv, kv_g_a, kv_w_ukv, kv_g_kn, kv_g_kr):
    batch, seq, d = x.shape
    m = batch * seq
    mem_len = mem.shape[1]
    mem_w = w_mem_kv.shape[-1] // 2
    conv_ch = a_conv_w.shape[-1]
    q_lora = b_g_qa.shape[-1]
    kv_lora = kv_g_a.shape[-1]
    nope, rope = b_g_qn.shape[-1], b_g_qr.shape[-1]
    n_heads = b_w_uq.shape[-1] // (nope + rope)
    dv = kv_w_ukv.shape[-1] // n_heads - nope
    assert 2 * rope == LANES and nope == LANES and dv == LANES
    assert (2 * conv_ch) % mem_w == 0 and q_lora % mem_w == 0
    assert b_w_in.shape[0] == 1 and a_w_in.shape[0] == 1 and norm_mix.shape[0] == 2

    cast = lambda w: w.astype(BF16)
    w_uq = b_w_uq[0].reshape(q_lora, n_heads, nope + rope)
    w_q_pad = cast(jnp.concatenate([w_uq, w_uq[:, :, nope:]], axis=-1).reshape(q_lora, n_heads * 2 * nope))
    w_ukv = kv_w_ukv.reshape(kv_lora, n_heads, nope + dv)
    w_kn = cast(w_ukv[:, :, :nope].reshape(kv_lora, n_heads * nope))
    w_v = cast(w_ukv[:, :, nope:].reshape(kv_lora, n_heads * dv))
    w_dkv_pad = jnp.concatenate([kv_w_dkv, kv_w_dkv[:, kv_lora:]], axis=-1)
    pad_gain = lambda g: jnp.concatenate([g, g]).reshape(1, 2 * rope)

    inv_freq = ROPE_THETA ** (-jnp.arange(0, rope, 2, dtype=F32) / rope)
    ang = positions.astype(F32).reshape(m, 1) * inv_freq
    zeros = jnp.zeros((m, LANES - rope), F32)
    cos_t = jnp.concatenate([jnp.cos(ang), jnp.cos(ang), zeros], axis=-1)
    sin_t = jnp.concatenate([-jnp.sin(ang), jnp.sin(ang), zeros], axis=-1)

    x = x.reshape(m, d)
    kv_mem = mem_kv(mem.reshape(batch * mem_len, d), norm_mem, w_mem_kv, g_mem_k)

    out_proj = functools.partial(matmul, out_dtype=F32, tm=1024, tn=512, emit_stats=True)
    h, = rmsnorm_cast(x, norm_mix[0:1])
    u = _wide([h], a_w_in[0], out_dtype=F32, name="a_in_proj")
    y_main = conformer_conv(u, a_conv_w[0], a_conv_b[0], a_ln_g[0], a_ln_b[0], seq=seq)
    y_mem = mem_attention(u, 2 * conv_ch // mem_w, kv_mem, 0, g_mem_q[0], seq=seq, mem_len=mem_len)
    x, xb, ssq = out_proj([y_main, y_mem], cast(a_w_out[0]), residual=x, name="a_out_proj")
    x, xb, ssq = _mlp(x, xb, ssq, norm_mlp[0], w_mlp_in, w_mlp_out, 0, emit_stats=True)

    c_kv, k_rope = kv_down(xb, ssq, kv_g_in, w_dkv_pad, kv_g_a, pad_gain(kv_g_kr), cos_t, sin_t)
    k_all, v_all = kv_up(c_kv, w_kn, w_v, k_rope, kv_g_kn, n_heads=n_heads)
    u = matmul([xb], b_w_in[0], norm=(norm_mix[1], ssq), out_dtype=F32, tm=1024, tn=512, name="b_in_proj")
    q_all = q_proj(u, b_g_qa[0], w_q_pad, b_g_qn[0], pad_gain(b_g_qr[0]), cos_t, sin_t,
                   n_heads=n_heads, scale=(nope + rope) ** -0.5 * math.log2(math.e))
    y_main = mla_attention(q_all, k_all, v_all, batch=batch, seq=seq, n_heads=n_heads)
    y_mem = mem_attention(u, q_lora // mem_w, kv_mem, 1, g_mem_q[1], seq=seq, mem_len=mem_len)
    x, xb, ssq = out_proj([y_main, y_mem], cast(b_w_out[0]), residual=x, name="b_out_proj")
    x = _mlp(x, xb, ssq, norm_mlp[1], w_mlp_in, w_mlp_out, 1, emit_stats=False)
    return x.reshape(batch, seq, d)
```

```python
import functools
import math

import jax
import jax.numpy as jnp
from jax import lax
from jax.experimental import pallas as pl
from jax.experimental.pallas import tpu as pltpu

EPS = 1e-6
ROPE_THETA = 10000.0
LANES = 128
VMEM_LIMIT_BYTES = 56 << 20
KSPLIT_STATS_VMEM_BYTES = 62 << 20
F32 = jnp.float32
BF16 = jnp.bfloat16


def _tile(n, pref, mult=LANES):
    if n <= pref:
        return n
    t = (pref // mult) * mult
    while t >= mult:
        if n % t == 0:
            return t
        t -= mult
    raise ValueError(f"no tile for {n} (pref {pref}, mult {mult})")


def _params(*semantics, vmem_limit_bytes=VMEM_LIMIT_BYTES):
    return pltpu.CompilerParams(dimension_semantics=semantics, vmem_limit_bytes=vmem_limit_bytes)


def _rms(x, width=None):
    width = x.shape[-1] if width is None else width
    ms = jnp.sum(x * x, axis=-1, keepdims=True) * (1.0 / width)
    return x * lax.rsqrt(ms + EPS)


def _rmsnorm_kernel(x_ref, g_ref, *o_refs):
    xn = _rms(x_ref[...])
    for i, o_ref in enumerate(o_refs):
        o_ref[...] = (xn * g_ref[i:i + 1, :]).astype(o_ref.dtype)


def rmsnorm_cast(x, gains):
    m, d = x.shape
    n = gains.shape[0]
    tm = _tile(m, 512, 8)
    return pl.pallas_call(
        _rmsnorm_kernel,
        grid=(m // tm,),
        in_specs=[pl.BlockSpec((tm, d), lambda i: (i, 0)),
                  pl.BlockSpec((n, d), lambda i: (0, 0))],
        out_specs=[pl.BlockSpec((tm, d), lambda i: (i, 0)) for _ in range(n)],
        out_shape=[jax.ShapeDtypeStruct((m, d), BF16) for _ in range(n)],
        compiler_params=_params("parallel"),
        name="rmsnorm_cast",
    )(x, gains)


def _row_stats(x, xb_ref, ssq_ref, first_col_tile, gain=None):
    xb_ref[...] = (x if gain is None else x * gain).astype(xb_ref.dtype)
    sq = x * x
    partial = functools.reduce(lambda p, q: p + q,
                               [sq[:, c * LANES:(c + 1) * LANES] for c in range(x.shape[1] // LANES)])

    @pl.when(first_col_tile)
    def _():
        ssq_ref[...] = partial

    @pl.when(jnp.logical_not(first_col_tile))
    def _():
        ssq_ref[...] += partial


def _matmul_kernel(*refs, names, act, norm_dim):
    r = dict(zip(names, refs))
    if "side_in" in r:
        r["side_out"][...] = r["side_in"][...].astype(BF16)
    acc = None
    for key in names:
        if not key.startswith("a"):
            continue
        w = r["w" + key[1:]][...]
        if "gain" in r:
            g = r["gain"][...]
            w = jnp.concatenate([w[:, c * LANES:(c + 1) * LANES] * g for c in range(w.shape[1] // LANES)], axis=1)
        part = jnp.dot(r[key][...], w.astype(BF16), preferred_element_type=F32)
        acc = part if acc is None else acc + part
    if "ssq" in r:
        acc = acc * lax.rsqrt(jnp.sum(r["ssq"][...], axis=-1, keepdims=True) * (1.0 / norm_dim) + EPS)
    if act == "relu2":
        acc = jnp.square(jnp.maximum(acc, 0.0))
    if "res" in r:
        acc = acc + r["res"][...]
    r["out"][...] = acc.astype(r["out"].dtype)
    if "xb" in r:
        _row_stats(acc, r["xb"], r["ssq_out"], pl.program_id(1) == 0,
                   gain=r["next_gain"][...] if "next_gain" in r else None)


def matmul(a_parts, w, *, out_dtype, act=None, residual=None, tm, tn, single_buffer_a=False, layer=None,
           side_cast=None, norm=None, emit_stats=False, next_gain=None, name):
    m = a_parts[0].shape[0]
    n = w.shape[-1]
    tm, tn = _tile(m, tm, 8), _tile(n, tn)
    nj = n // tn
    a_mode = dict(pipeline_mode=pl.Buffered(1)) if single_buffer_a else {}
    names, in_specs, args = [], [], []

    def add(name_, spec, arr):
        names.append(name_)
        in_specs.append(spec)
        args.append(arr)

    offset = 0
    for p, a in enumerate(a_parts):
        kp = a.shape[1]
        assert offset % kp == 0
        add(f"a{p}", pl.BlockSpec((tm, kp), lambda i, j: (i, 0), **a_mode), a)
        if layer is None:
            add(f"w{p}", pl.BlockSpec((kp, tn), lambda i, j, blk=offset // kp: (blk, j)), w)
        else:
            add(f"w{p}", pl.BlockSpec((None, kp, tn), lambda i, j, blk=offset // kp: (layer, blk, j)), w)
        offset += kp
    kdim = offset
    assert kdim == w.shape[-2]
    if norm is not None:
        gain, ssq = norm
        assert len(a_parts) == 1
        if gain is not None:
            assert w.dtype == F32
            add("gain", pl.BlockSpec((kdim, LANES), lambda i, j: (0, 0)),
                jnp.broadcast_to(gain.reshape(kdim, 1), (kdim, LANES)))
        add("ssq", pl.BlockSpec((tm, LANES), lambda i, j: (i, 0)), ssq)
    if next_gain is not None:
        assert emit_stats
        add("next_gain", pl.BlockSpec((1, tn), lambda i, j: (0, j)), next_gain.reshape(1, n))
    if residual is not None:
        add("res", pl.BlockSpec((tm, tn), lambda i, j: (i, j)), residual)
    if side_cast is not None:
        src, src_layer = side_cast
        slab = src.shape[1] // ((m // tm) * nj)
        add("side_in", pl.BlockSpec((None, slab, src.shape[2]), lambda i, j: (src_layer, i * nj + j, 0)), src)
    names.append("out")
    out_specs = [pl.BlockSpec((tm, tn), lambda i, j: (i, j))]
    out_shape = [jax.ShapeDtypeStruct((m, n), out_dtype)]
    if side_cast is not None:
        names.append("side_out")
        out_specs.append(pl.BlockSpec((slab, src.shape[2]), lambda i, j: (i * nj + j, 0)))
        out_shape.append(jax.ShapeDtypeStruct(src.shape[1:], BF16))
    if emit_stats:
        assert out_dtype == F32
        names += ["xb", "ssq_out"]
        out_specs += [pl.BlockSpec((tm, tn), lambda i, j: (i, j)), pl.BlockSpec((tm, LANES), lambda i, j: (i, 0))]
        out_shape += [jax.ShapeDtypeStruct((m, n), BF16), jax.ShapeDtypeStruct((m, LANES), F32)]
    outs = pl.pallas_call(
        functools.partial(_matmul_kernel, names=tuple(names), act=act, norm_dim=kdim),
        grid=(m // tm, nj),
        in_specs=in_specs,
        out_specs=out_specs,
        out_shape=out_shape,
        compiler_params=_params("parallel", "arbitrary"),
        name=name,
    )(*args)
    return outs if len(outs) > 1 else outs[0]


def side_cast_fits(rows, m, n, tm, tn):
    return slab_rows(rows, (m // _tile(m, tm, 8)) * (n // _tile(n, tn))) is not None


def slab_rows(rows, steps):
    if steps <= 0 or rows % steps or (rows // steps) % 16:
        return None
    return rows // steps


def _matmul_ksplit_kernel(a_ref, w_ref, r_ref, o_ref, *stats_refs):
    k = pl.program_id(2)

    @pl.when(k == 0)
    def _():
        o_ref[...] = r_ref[...]

    o_ref[...] += jnp.dot(a_ref[...], w_ref[...], preferred_element_type=F32)
    if stats_refs:
        @pl.when(k == pl.num_programs(2) - 1)
        def _():
            _row_stats(o_ref[...], *stats_refs, pl.program_id(1) == 0)


def matmul_ksplit(a, w, residual, *, tm, tn, tk, emit_stats=False, name):
    m, kdim = a.shape
    n = w.shape[1]
    tm, tn, tk = _tile(m, tm, 8), _tile(n, tn), _tile(kdim, tk)
    nj = n // tn
    out_specs = [pl.BlockSpec((tm, tn), lambda i, j, k: (i, j))]
    out_shape = [jax.ShapeDtypeStruct((m, n), F32)]
    if emit_stats:
        out_specs += [pl.BlockSpec((tm, tn), lambda i, j, k: (i, j)),
                      pl.BlockSpec((tm, LANES), lambda i, j, k: (i, 0))]
        out_shape += [jax.ShapeDtypeStruct((m, n), BF16), jax.ShapeDtypeStruct((m, LANES), F32)]
    outs = pl.pallas_call(
        _matmul_ksplit_kernel,
        grid=(m // tm, nj, kdim // tk),
        in_specs=[pl.BlockSpec((tm, tk), lambda i, j, k: (i, k)),
                  pl.BlockSpec((tk, tn), lambda i, j, k: (k, j)),
                  pl.BlockSpec((tm, tn), lambda i, j, k: (i, j))],
        out_specs=out_specs,
        out_shape=out_shape,
        compiler_params=_params("parallel", "arbitrary", "arbitrary",
                                vmem_limit_bytes=KSPLIT_STATS_VMEM_BYTES if emit_stats else VMEM_LIMIT_BYTES),
        name=name,
    )(a, w, residual)
    return outs if emit_stats else outs[0]


SUBLANES = 8
CONV_ROWS = 64
CONV_HALO = 32
CONV_TS = 256
CONV_LANES = 256


def _sigmoid(x):
    return 0.5 * jnp.tanh(0.5 * x) + 0.5


def _conv_kernel(*refs, ts, seq, cw, lc, has_side):
    if has_side:
        a_ref, gate_ref, w_ref, cb_ref, lng_ref, lnb_ref, side_in, o_ref, side_out, buf_ref, c_ref = refs
        side_out[...] = side_in[...].astype(BF16)
    else:
        a_ref, gate_ref, w_ref, cb_ref, lng_ref, lnb_ref, o_ref, buf_ref, c_ref = refs
    i = pl.program_id(0)
    at_seq_start = (i * ts) % seq == 0

    @pl.when(at_seq_start)
    def _():
        buf_ref[0:CONV_HALO, :] = jnp.zeros((CONV_HALO, buf_ref.shape[1]), F32)

    @pl.when(jnp.logical_not(at_seq_start))
    def _():
        buf_ref[0:CONV_HALO, :] = buf_ref[ts:ts + CONV_HALO, :]

    buf_ref[CONV_HALO:CONV_HALO + ts, :] = a_ref[...] * _sigmoid(gate_ref[...])
    lead = CONV_HALO - (cw - 1)
    n_ch = a_ref.shape[1]

    def chunk(r, carry):
        r0 = pl.multiple_of(r * CONV_ROWS, CONV_ROWS)
        for c0 in range(0, n_ch, lc):
            cols = slice(c0, c0 + lc)
            acc = None
            for phase in range(SUBLANES):
                taps = [j for j in range(cw) if (lead + j) % SUBLANES == phase]
                if not taps:
                    continue
                rows = CONV_ROWS + (SUBLANES if phase else 0)
                part = None
                for j in taps:
                    base = (lead + j) // SUBLANES * SUBLANES
                    x = buf_ref[pl.ds(r0 + base, rows), cols].reshape(rows // SUBLANES, SUBLANES, lc)
                    term = x * w_ref[j, :, cols][None]
                    part = term if part is None else part + term
                part = part.reshape(rows, lc)
                piece = part[phase:phase + CONV_ROWS] if phase else part
                acc = piece if acc is None else acc + piece
            c_ref[pl.ds(r0, CONV_ROWS), cols] = acc + cb_ref[:, cols]
        c = c_ref[pl.ds(r0, CONV_ROWS), :]
        mu = jnp.mean(c, axis=-1, keepdims=True)
        cc = c - mu
        var = jnp.mean(cc * cc, axis=-1, keepdims=True)
        y = cc * lax.rsqrt(var + EPS) * lng_ref[...] + lnb_ref[...]
        o_ref[pl.ds(r0, CONV_ROWS), :] = (y * _sigmoid(y)).astype(o_ref.dtype)
        return carry

    lax.fori_loop(0, ts // CONV_ROWS, chunk, 0)


def conformer_conv(u, conv_w, conv_b, ln_g, ln_b, *, seq, side_cast=None):
    m = u.shape[0]
    cw, c = conv_w.shape
    assert cw - 1 <= CONV_HALO
    ts = _tile(seq, CONV_TS, CONV_HALO)
    lc = _tile(c, CONV_LANES)
    row = lambda v: v.reshape(1, c)
    in_specs = [pl.BlockSpec((ts, c), lambda i: (i, 0)),
                pl.BlockSpec((ts, c), lambda i: (i, 1)),
                pl.BlockSpec((cw, SUBLANES, c), lambda i: (0, 0, 0)),
                pl.BlockSpec((1, c), lambda i: (0, 0)),
                pl.BlockSpec((1, c), lambda i: (0, 0)),
                pl.BlockSpec((1, c), lambda i: (0, 0))]
    args = [u, u, jnp.broadcast_to(conv_w[:, None, :], (cw, SUBLANES, c)), row(conv_b), row(ln_g), row(ln_b)]
    out_specs = [pl.BlockSpec((ts, c), lambda i: (i, 0))]
    out_shape = [jax.ShapeDtypeStruct((m, c), BF16)]
    if side_cast is not None:
        src, src_layer = side_cast
        slab = slab_rows(src.shape[1], m // ts)
        in_specs.append(pl.BlockSpec((None, slab, src.shape[2]), lambda i: (src_layer, i, 0)))
        args.append(src)
        out_specs.append(pl.BlockSpec((slab, src.shape[2]), lambda i: (i, 0)))
        out_shape.append(jax.ShapeDtypeStruct(src.shape[1:], BF16))
    outs = pl.pallas_call(
        functools.partial(_conv_kernel, ts=ts, seq=seq, cw=cw, lc=lc, has_side=side_cast is not None),
        grid=(m // ts,),
        in_specs=in_specs,
        out_specs=out_specs,
        out_shape=out_shape,
        scratch_shapes=[pltpu.VMEM((CONV_HALO + ts, c), F32), pltpu.VMEM((ts, c), F32)],
        compiler_params=_params("arbitrary"),
        name="conformer_conv",
    )(*args)
    return outs if side_cast is not None else outs[0]


def _mem_kv_kernel(mem_ref, gn_ref, w_ref, gk_ref, o_ref, *, n_k_heads):
    j = pl.program_id(1)
    hn = (_rms(mem_ref[...]) * gn_ref[0]).astype(BF16)
    kv = jnp.dot(hn, w_ref[0].astype(BF16), preferred_element_type=F32)
    kn = _rms(kv) * gk_ref[0]
    o_ref[0] = jnp.where(j < n_k_heads, kn, kv).astype(o_ref.dtype)


def mem_kv(mem2d, norm_mem, w_mem_kv, g_mem_k):
    nl, d, n2 = w_mem_kv.shape
    hd = g_mem_k.shape[-1]
    bm = mem2d.shape[0]
    return pl.pallas_call(
        functools.partial(_mem_kv_kernel, n_k_heads=n2 // 2 // hd),
        grid=(nl, n2 // hd),
        in_specs=[pl.BlockSpec((bm, d), lambda l, j: (0, 0)),
                  pl.BlockSpec((1, 1, d), lambda l, j: (l, 0, 0)),
                  pl.BlockSpec((1, d, hd), lambda l, j: (l, 0, j)),
                  pl.BlockSpec((1, 1, hd), lambda l, j: (l, 0, 0))],
        out_specs=pl.BlockSpec((1, bm, hd), lambda l, j: (l, 0, j)),
        out_shape=jax.ShapeDtypeStruct((nl, bm, n2), BF16),
        compiler_params=_params("parallel", "parallel"),
        name="mem_kv",
    )(mem2d, norm_mem.reshape(nl, 1, d), w_mem_kv, g_mem_k.reshape(nl, 1, hd))


def _mem_attn_kernel(q_ref, k_ref, v_ref, gq_ref, o_ref, *, n_heads, hd):
    scale = hd ** -0.5
    for h in range(n_heads):
        cols = slice(h * hd, (h + 1) * hd)
        q = (_rms(q_ref[:, cols]) * (gq_ref[...] * scale)).astype(BF16)
        s = lax.dot_general(q, k_ref[0, :, cols], (((1,), (1,)), ((), ())), preferred_element_type=F32)
        p = jnp.exp(s - jnp.max(s, axis=-1, keepdims=True))
        denom = jnp.sum(p, axis=-1, keepdims=True)
        o = jnp.dot(p.astype(BF16), v_ref[0, :, cols], preferred_element_type=F32)
        o_ref[:, cols] = (o / denom).astype(o_ref.dtype)


def mem_attention(qsrc, q_col_block, kv, layer, g_q, *, seq, mem_len):
    m = qsrc.shape[0]
    w = kv.shape[-1] // 2
    hd = g_q.shape[-1]
    ts = _tile(seq, 512, 8)
    tiles_per_seq = seq // ts
    return pl.pallas_call(
        functools.partial(_mem_attn_kernel, n_heads=w // hd, hd=hd),
        grid=(m // ts,),
        in_specs=[pl.BlockSpec((ts, w), lambda i: (i, q_col_block)),
                  pl.BlockSpec((1, mem_len, w), lambda i: (layer, i // tiles_per_seq, 0)),
                  pl.BlockSpec((1, mem_len, w), lambda i: (layer, i // tiles_per_seq, 1)),
                  pl.BlockSpec((1, hd), lambda i: (0, 0))],
        out_specs=pl.BlockSpec((ts, w), lambda i: (i, 0)),
        out_shape=jax.ShapeDtypeStruct((m, w), BF16),
        compiler_params=_params("parallel"),
        name="mem_attention",
    )(qsrc, kv, kv, g_q.reshape(1, hd))


def _rope(t, cos_ref, sin_ref):
    partner = pltpu.roll(t, LANES // 4, 1)
    return t * cos_ref[...] + partner * sin_ref[...]


def _dkv_kernel(xb_ref, ssq_ref, gain_ref, w_ref, ga_ref, gr_ref, cos_ref, sin_ref, ckv_ref, kr_ref, *, lora, dim):
    g = gain_ref[...]
    w = w_ref[...]
    w = jnp.concatenate([w[:, c * LANES:(c + 1) * LANES] * g for c in range(w.shape[1] // LANES)], axis=1)
    ckr = jnp.dot(xb_ref[...], w.astype(BF16), preferred_element_type=F32)
    ckr = ckr * lax.rsqrt(jnp.sum(ssq_ref[...], axis=-1, keepdims=True) * (1.0 / dim) + EPS)
    ckv_ref[...] = (_rms(ckr[:, :lora]) * ga_ref[...]).astype(ckv_ref.dtype)
    kr = _rms(ckr[:, lora:]) * gr_ref[...]
    kr_ref[...] = _rope(kr, cos_ref, sin_ref).astype(kr_ref.dtype)


def kv_down(xb, ssq, gain, w_dkv_pad, g_a, g_kr_pad, cos_t, sin_t):
    m, d = xb.shape
    lora = g_a.shape[-1]
    tm = _tile(m, 1024, 8)
    full = lambda shape: pl.BlockSpec(shape, lambda i: (0, 0))
    rows = lambda width: pl.BlockSpec((tm, width), lambda i: (i, 0))
    return pl.pallas_call(
        functools.partial(_dkv_kernel, lora=lora, dim=d),
        grid=(m // tm,),
        in_specs=[rows(d), rows(LANES), full((d, LANES)), full(w_dkv_pad.shape), full((1, lora)), full((1, LANES)),
                  rows(LANES), rows(LANES)],
        out_specs=[rows(lora), rows(LANES)],
        out_shape=[jax.ShapeDtypeStruct((m, lora), BF16), jax.ShapeDtypeStruct((m, LANES), BF16)],
        compiler_params=_params("parallel"),
        name="kv_down",
    )(xb, ssq, jnp.broadcast_to(gain.reshape(d, 1), (d, LANES)), w_dkv_pad, g_a.reshape(1, lora), g_kr_pad,
      cos_t, sin_t)


HEAD_PAIR = 2


def _ukv_kernel(c_ref, wk_ref, wv_ref, kr_ref, gk_ref, k_ref, v_ref, *, heads, nope):
    c = c_ref[...]
    kr = kr_ref[...]
    width = HEAD_PAIR * nope
    for pair in range(heads // HEAD_PAIR):
        kn = jnp.dot(c, wk_ref[:, pair * width:(pair + 1) * width], preferred_element_type=F32)
        for h in range(HEAD_PAIR):
            base = (pair * HEAD_PAIR + h) * 2 * nope
            k_ref[:, base:base + nope] = (_rms(kn[:, h * nope:(h + 1) * nope]) * gk_ref[...]).astype(k_ref.dtype)
            k_ref[:, base + nope:base + 2 * nope] = kr
        cols = slice(pair * width, (pair + 1) * width)
        v_ref[:, cols] = jnp.dot(c, wv_ref[:, cols], preferred_element_type=F32).astype(v_ref.dtype)


def kv_up(c_kv, w_kn, w_v, kr, g_kn, *, n_heads):
    m, lora = c_kv.shape
    nope = g_kn.shape[-1]
    dv = w_v.shape[1] // n_heads
    assert nope == LANES and dv == LANES and n_heads % HEAD_PAIR == 0
    tm = _tile(m, 512, 8)
    whole = lambda shape: pl.BlockSpec(shape, lambda i: (0, 0))
    rows = lambda width: pl.BlockSpec((tm, width), lambda i: (i, 0))
    return pl.pallas_call(
        functools.partial(_ukv_kernel, heads=n_heads, nope=nope),
        grid=(m // tm,),
        in_specs=[rows(lora), whole(w_kn.shape), whole(w_v.shape), rows(LANES), whole((1, nope))],
        out_specs=[rows(n_heads * 2 * nope), rows(n_heads * dv)],
        out_shape=[jax.ShapeDtypeStruct((m, n_heads * 2 * nope), BF16),
                   jax.ShapeDtypeStruct((m, n_heads * dv), BF16)],
        compiler_params=_params("parallel"),
        name="kv_up",
    )(c_kv, w_kn, w_v, kr, g_kn.reshape(1, nope))


def _q_kernel(u_ref, ga_ref, w_ref, gn_ref, gr_ref, cos_ref, sin_ref, q_ref, cq_ref, *, heads, nope, scale):
    cq_ref[...] = (_rms(u_ref[...]) * ga_ref[...]).astype(cq_ref.dtype)
    width = HEAD_PAIR * 2 * nope
    for pair in range(heads // HEAD_PAIR):
        q = jnp.dot(cq_ref[...], w_ref[:, pair * width:(pair + 1) * width], preferred_element_type=F32)
        for h in range(HEAD_PAIR):
            base = h * 2 * nope
            out = pair * width + base
            qn = _rms(q[:, base:base + nope]) * (gn_ref[...] * scale)
            q_ref[:, out:out + nope] = qn.astype(q_ref.dtype)
            qr = _rms(q[:, base + nope:base + 2 * nope]) * (gr_ref[...] * scale)
            q_ref[:, out + nope:out + 2 * nope] = _rope(qr, cos_ref, sin_ref).astype(q_ref.dtype)


def q_proj(u, g_qa, w_q_pad, g_qn, g_qr_pad, cos_t, sin_t, *, n_heads, scale):
    m = u.shape[0]
    lora = g_qa.shape[-1]
    nope = g_qn.shape[-1]
    assert nope == LANES and n_heads % HEAD_PAIR == 0
    tm = _tile(m, 512, 8)
    whole = lambda shape: pl.BlockSpec(shape, lambda i: (0, 0))
    rows = lambda width: pl.BlockSpec((tm, width), lambda i: (i, 0))
    return pl.pallas_call(
        functools.partial(_q_kernel, heads=n_heads, nope=nope, scale=scale),
        grid=(m // tm,),
        in_specs=[rows(lora), whole((1, lora)), whole(w_q_pad.shape), whole((1, nope)), whole((1, LANES)),
                  rows(LANES), rows(LANES)],
        out_specs=rows(n_heads * 2 * nope),
        out_shape=jax.ShapeDtypeStruct((m, n_heads * 2 * nope), BF16),
        scratch_shapes=[pltpu.VMEM((tm, lora), BF16)],
        compiler_params=_params("parallel"),
        name="q_proj",
    )(u, g_qa.reshape(1, lora), w_q_pad, g_qn.reshape(1, nope), g_qr_pad, cos_t, sin_t)


ATT_TQ = 2048
ATT_TK = 512
ATT_GROUP = 4


def _mla_kernel(*refs, tq, tk, group, has_side):
    if has_side:
        q_ref, k_ref, v_ref, side_in, o_ref, side_out, m_ref, l_ref, acc_ref = refs
        side_out[...] = side_in[...].astype(BF16)
    else:
        q_ref, k_ref, v_ref, o_ref, m_ref, l_ref, acc_ref = refs
    i = pl.program_id(2)
    m_ref[...] = jnp.full(m_ref.shape, -jnp.inf, F32)
    l_ref[...] = jnp.zeros(l_ref.shape, F32)
    acc_ref[...] = jnp.zeros(acc_ref.shape, F32)
    chunks = tk // LANES

    def sub_block(start, diag_offset=None):
        rows = slice(0 if diag_offset is None else diag_offset, tq)
        s = lax.dot_general(q_ref[rows, :], k_ref[pl.ds(start, tk), :], (((1,), (1,)), ((), ())),
                            preferred_element_type=F32)
        if diag_offset is not None:
            row = lax.broadcasted_iota(jnp.int32, s.shape, 0)
            col = lax.broadcasted_iota(jnp.int32, s.shape, 1)
            s = jnp.where(col <= row, s, -jnp.inf)
        m_prev = m_ref[rows, :]
        m_new = jnp.maximum(m_prev, jnp.max(s, axis=-1, keepdims=True))
        alpha = jnp.exp2(m_prev - m_new)
        p = [jnp.exp2(s[:, c * LANES:(c + 1) * LANES] - m_new) for c in range(chunks)]
        l_ref[rows, :] = alpha * l_ref[rows, :] + functools.reduce(lambda a, b: a + b, p)
        pv = jnp.dot(jnp.concatenate(p, axis=1).astype(BF16), v_ref[pl.ds(start, tk), :],
                     preferred_element_type=F32)
        acc_ref[rows, :] = alpha * acc_ref[rows, :] + pv
        m_ref[rows, :] = m_new

    per_tile = tq // tk
    n_off = i * per_tile
    n_groups = n_off // group

    def body(g, carry):
        for t in range(group):
            sub_block(pl.multiple_of((g * group + t) * tk, tk))
        return carry

    lax.fori_loop(0, n_groups, body, 0)
    step = per_tile
    while group % step:
        step -= 1
    for rem in range(step, group, step):
        @pl.when(n_off % group == rem)
        def _(rem=rem):
            for t in range(rem):
                sub_block(pl.multiple_of((n_groups * group + t) * tk, tk))
    for t in range(per_tile):
        sub_block(pl.multiple_of(i * tq + t * tk, tk), diag_offset=t * tk)
    denom = jnp.sum(l_ref[...], axis=-1, keepdims=True)
    o_ref[...] = (acc_ref[...] / denom).astype(o_ref.dtype)


def attention_cast_heads(rows, batch, seq, n_heads):
    nq = seq // _tile(seq, ATT_TQ, ATT_TK)
    for heads in range(n_heads, 0, -1):
        if slab_rows(rows, batch * heads * nq) is not None:
            return heads
    return 0


def mla_attention(q, k, v, *, batch, seq, n_heads, side_cast=None):
    m = q.shape[0]
    dqk = q.shape[1] // n_heads
    dv = v.shape[1] // n_heads
    assert dv == LANES
    tq = _tile(seq, ATT_TQ, ATT_TK)
    tk = min(ATT_TK, tq)
    nq = seq // tq
    in_specs = [pl.BlockSpec((tq, dqk), lambda b, h, i: (b * nq + i, h)),
                pl.BlockSpec((seq, dqk), lambda b, h, i: (b, h)),
                pl.BlockSpec((seq, dv), lambda b, h, i: (b, h))]
    args = [q, k, v]
    out_specs = [pl.BlockSpec((tq, dv), lambda b, h, i: (b * nq + i, h))]
    out_shape = [jax.ShapeDtypeStruct((m, n_heads * dv), BF16)]
    if side_cast is not None:
        src, src_layer = side_cast
        heads = attention_cast_heads(src.shape[1], batch, seq, n_heads)
        slab = slab_rows(src.shape[1], batch * heads * nq)

        def slab_index(b, h, i):
            done = h >= heads
            return (b * heads + jnp.minimum(h, heads - 1)) * nq + jnp.where(done, nq - 1, i)

        in_specs.append(pl.BlockSpec((None, slab, src.shape[2]), lambda b, h, i: (src_layer, slab_index(b, h, i), 0)))
        args.append(src)
        out_specs.append(pl.BlockSpec((slab, src.shape[2]), lambda b, h, i: (slab_index(b, h, i), 0)))
        out_shape.append(jax.ShapeDtypeStruct(src.shape[1:], BF16))
    outs = pl.pallas_call(
        functools.partial(_mla_kernel, tq=tq, tk=tk, group=ATT_GROUP, has_side=side_cast is not None),
        grid=(batch, n_heads, nq),
        in_specs=in_specs,
        out_specs=out_specs,
        out_shape=out_shape,
        scratch_shapes=[pltpu.VMEM((tq, LANES), F32), pltpu.VMEM((tq, LANES), F32), pltpu.VMEM((tq, dv), F32)],
        compiler_params=_params("parallel", "arbitrary", "arbitrary"),
        name="mla_attention",
    )(*args)
    return outs if side_cast is not None else outs[0]


WIDE_TM, WIDE_TN = 2048, 512


def _wide(a_parts, w_f32, **kw):
    return matmul(a_parts, w_f32, tm=WIDE_TM, tn=WIDE_TN, single_buffer_a=True, **kw)


MLP_IN_TM, MLP_IN_TN = 1024, 1024


def _mlp(x, xg, ssq, w_in_bf16, w_out, layer, emit_stats):
    kw = dict(norm=(None, ssq), out_dtype=BF16, act="relu2", tm=MLP_IN_TM, tn=MLP_IN_TN, name="mlp_in")
    if side_cast_fits(w_out.shape[1], x.shape[0], w_in_bf16.shape[-1], MLP_IN_TM, MLP_IN_TN):
        hm, w_out_bf16 = matmul([xg], w_in_bf16, side_cast=(w_out, layer), **kw)
    else:
        hm, w_out_bf16 = matmul([xg], w_in_bf16, **kw), w_out[layer].astype(BF16)
    return matmul_ksplit(hm, w_out_bf16, x, tm=1024, tn=1024, tk=4096, emit_stats=emit_stats, name="mlp_out")


def kernel(x, mem, positions, norm_mix, norm_mlp, norm_mem, w_mem_kv, g_mem_q, g_mem_k, w_mlp_in, w_mlp_out,
           a_w_in, a_conv_w, a_conv_b, a_ln_g, a_ln_b, a_w_out, b_w_in, b_g_qa, b_w_uq, b_g_qn, b_g_qr, b_w_out,
           kv_g_in, kv_w_dkv, kv_g_a, kv_w_ukv, kv_g_kn, kv_g_kr):
    batch, seq, d = x.shape
    m = batch * seq
    mem_len = mem.shape[1]
    mem_w = w_mem_kv.shape[-1] // 2
    conv_ch = a_conv_w.shape[-1]
    q_lora = b_g_qa.shape[-1]
    kv_lora = kv_g_a.shape[-1]
    nope, rope = b_g_qn.shape[-1], b_g_qr.shape[-1]
    n_heads = b_w_uq.shape[-1] // (nope + rope)
    dv = kv_w_ukv.shape[-1] // n_heads - nope
    assert 2 * rope == LANES and nope == LANES and dv == LANES
    assert (2 * conv_ch) % mem_w == 0 and q_lora % mem_w == 0
    assert b_w_in.shape[0] == 1 and a_w_in.shape[0] == 1 and norm_mix.shape[0] == 2

    cast = lambda w: w.astype(BF16)
    w_uq = b_w_uq[0].reshape(q_lora, n_heads, nope + rope)
    w_q_pad = cast(jnp.concatenate([w_uq, w_uq[:, :, nope:]], axis=-1).reshape(q_lora, n_heads * 2 * nope))
    w_ukv = kv_w_ukv.reshape(kv_lora, n_heads, nope + dv)
    w_kn = cast(w_ukv[:, :, :nope].reshape(kv_lora, n_heads * nope))
    w_v = cast(w_ukv[:, :, nope:].reshape(kv_lora, n_heads * dv))
    w_dkv_pad = jnp.concatenate([kv_w_dkv, kv_w_dkv[:, kv_lora:]], axis=-1)
    pad_gain = lambda g: jnp.concatenate([g, g]).reshape(1, 2 * rope)

    inv_freq = ROPE_THETA ** (-jnp.arange(0, rope, 2, dtype=F32) / rope)
    ang = positions.astype(F32).reshape(m, 1) * inv_freq
    zeros = jnp.zeros((m, LANES - rope), F32)
    cos_t = jnp.concatenate([jnp.cos(ang), jnp.cos(ang), zeros], axis=-1)
    sin_t = jnp.concatenate([-jnp.sin(ang), jnp.sin(ang), zeros], axis=-1)

    x = x.reshape(m, d)
    kv_mem = mem_kv(mem.reshape(batch * mem_len, d), norm_mem, w_mem_kv, g_mem_k)

    out_proj = functools.partial(matmul, out_dtype=F32, tm=1024, tn=512, emit_stats=True)
    h, = rmsnorm_cast(x, norm_mix[0:1])
    u = _wide([h], a_w_in[0], out_dtype=F32, name="a_in_proj")
    conv_args = (u, a_conv_w[0], a_conv_b[0], a_ln_g[0], a_ln_b[0])
    if slab_rows(w_mlp_in.shape[1], m // _tile(seq, CONV_TS, CONV_HALO)) is not None:
        y_main, w_in0 = conformer_conv(*conv_args, seq=seq, side_cast=(w_mlp_in, 0))
    else:
        y_main, w_in0 = conformer_conv(*conv_args, seq=seq), cast(w_mlp_in[0])
    y_mem = mem_attention(u, 2 * conv_ch // mem_w, kv_mem, 0, g_mem_q[0], seq=seq, mem_len=mem_len)
    x, xg, ssq = out_proj([y_main, y_mem], cast(a_w_out[0]), residual=x, next_gain=norm_mlp[0], name="a_out_proj")
    x, xb, ssq = _mlp(x, xg, ssq, w_in0, w_mlp_out, 0, emit_stats=True)

    c_kv, k_rope = kv_down(xb, ssq, kv_g_in, w_dkv_pad, kv_g_a, pad_gain(kv_g_kr), cos_t, sin_t)
    k_all, v_all = kv_up(c_kv, w_kn, w_v, k_rope, kv_g_kn, n_heads=n_heads)
    u = matmul([xb], b_w_in[0], norm=(norm_mix[1], ssq), out_dtype=F32, tm=1024, tn=512, name="b_in_proj")
    q_all = q_proj(u, b_g_qa[0], w_q_pad, b_g_qn[0], pad_gain(b_g_qr[0]), cos_t, sin_t,
                   n_heads=n_heads, scale=(nope + rope) ** -0.5 * math.log2(math.e))
    if attention_cast_heads(w_mlp_in.shape[1], batch, seq, n_heads):
        y_main, w_in1 = mla_attention(q_all, k_all, v_all, batch=batch, seq=seq, n_heads=n_heads,
                                      side_cast=(w_mlp_in, 1))
    else:
        y_main = mla_attention(q_all, k_all, v_all, batch=batch, seq=seq, n_heads=n_heads)
        w_in1 = cast(w_mlp_in[1])
    y_mem = mem_attention(u, q_lora // mem_w, kv_mem, 1, g_mem_q[1], seq=seq, mem_len=mem_len)
    x, xg, ssq = out_proj([y_main, y_mem], cast(b_w_out[0]), residual=x, next_gain=norm_mlp[1], name="b_out_proj")
    x = _mlp(x, xg, ssq, w_in1, w_mlp_out, 1, emit_stats=False)
    return x.reshape(batch, seq, d)
```

```python
import functools
import math

import jax
import jax.numpy as jnp
from jax import lax
from jax.experimental import pallas as pl
from jax.experimental.pallas import tpu as pltpu

EPS = 1e-6
ROPE_THETA = 10000.0
LANES = 128
SUBLANES = 8
BF16_SUBLANES = 16
MIB = 1 << 20
VMEM_LIMIT_BYTES = 56 * MIB
KSPLIT_STATS_VMEM_BYTES = 62 * MIB
F32 = jnp.float32
BF16 = jnp.bfloat16


def _tile(n, pref, mult=LANES):
    if n <= pref:
        return n
    t = (pref // mult) * mult
    while t >= mult:
        if n % t == 0:
            return t
        t -= mult
    raise ValueError(f"no tile for {n} (pref {pref}, mult {mult})")


def _params(*semantics, vmem_limit_bytes=VMEM_LIMIT_BYTES):
    return pltpu.CompilerParams(dimension_semantics=semantics, vmem_limit_bytes=vmem_limit_bytes)


def _rms(x):
    ms = jnp.sum(x * x, axis=-1, keepdims=True) * (1.0 / x.shape[-1])
    return x * lax.rsqrt(ms + EPS)


def _rmsnorm_kernel(x_ref, g_ref, *o_refs):
    xn = _rms(x_ref[...])
    for i, o_ref in enumerate(o_refs):
        o_ref[...] = (xn * g_ref[i:i + 1, :]).astype(o_ref.dtype)


def rmsnorm_cast(x, gains):
    m, d = x.shape
    n = gains.shape[0]
    tm = _tile(m, 512, SUBLANES)
    return pl.pallas_call(
        _rmsnorm_kernel,
        grid=(m // tm,),
        in_specs=[pl.BlockSpec((tm, d), lambda i: (i, 0)),
                  pl.BlockSpec((n, d), lambda i: (0, 0))],
        out_specs=[pl.BlockSpec((tm, d), lambda i: (i, 0)) for _ in range(n)],
        out_shape=[jax.ShapeDtypeStruct((m, d), BF16) for _ in range(n)],
        compiler_params=_params("parallel"),
        name="rmsnorm_cast",
    )(x, gains)


def _row_stats(x, xb_ref, ssq_ref, first_col_tile, gain=None):
    xb_ref[...] = (x if gain is None else x * gain).astype(xb_ref.dtype)
    sq = x * x
    partial = functools.reduce(lambda p, q: p + q,
                               [sq[:, c * LANES:(c + 1) * LANES] for c in range(x.shape[1] // LANES)])

    @pl.when(first_col_tile)
    def _():
        ssq_ref[...] = partial

    @pl.when(jnp.logical_not(first_col_tile))
    def _():
        ssq_ref[...] += partial


def _matmul_kernel(*refs, names, act, norm_dim):
    r = dict(zip(names, refs))
    if "side_in" in r:
        r["side_out"][...] = r["side_in"][...].astype(BF16)
    acc = None
    for key in names:
        if not key.startswith("a"):
            continue
        w = r["w" + key[1:]][...]
        if "gain" in r:
            g = r["gain"][...]
            w = jnp.concatenate([w[:, c * LANES:(c + 1) * LANES] * g for c in range(w.shape[1] // LANES)], axis=1)
        part = jnp.dot(r[key][...], w.astype(BF16), preferred_element_type=F32)
        acc = part if acc is None else acc + part
    if "ssq" in r:
        acc = acc * lax.rsqrt(jnp.sum(r["ssq"][...], axis=-1, keepdims=True) * (1.0 / norm_dim) + EPS)
    if act == "relu2":
        acc = jnp.square(jnp.maximum(acc, 0.0))
    if "res" in r:
        acc = acc + r["res"][...]
    r["out"][...] = acc.astype(r["out"].dtype)
    if "xb" in r:
        _row_stats(acc, r["xb"], r["ssq_out"], pl.program_id(1) == 0,
                   gain=r["next_gain"][...] if "next_gain" in r else None)


def matmul(a_parts, w, *, out_dtype, act=None, residual=None, tm, tn, single_buffer_a=False,
           side_cast=None, norm=None, emit_stats=False, next_gain=None, name):
    m = a_parts[0].shape[0]
    n = w.shape[-1]
    tm, tn = _tile(m, tm, SUBLANES), _tile(n, tn)
    nj = n // tn
    a_mode = dict(pipeline_mode=pl.Buffered(1)) if single_buffer_a else {}
    names, in_specs, args = [], [], []

    def add(name_, spec, arr):
        names.append(name_)
        in_specs.append(spec)
        args.append(arr)

    offset = 0
    for p, a in enumerate(a_parts):
        kp = a.shape[1]
        assert offset % kp == 0
        add(f"a{p}", pl.BlockSpec((tm, kp), lambda i, j: (i, 0), **a_mode), a)
        add(f"w{p}", pl.BlockSpec((kp, tn), lambda i, j, blk=offset // kp: (blk, j)), w)
        offset += kp
    kdim = offset
    assert kdim == w.shape[-2]
    if norm is not None:
        gain, ssq = norm
        assert len(a_parts) == 1
        if gain is not None:
            assert w.dtype == F32
            add("gain", pl.BlockSpec((kdim, LANES), lambda i, j: (0, 0)),
                jnp.broadcast_to(gain.reshape(kdim, 1), (kdim, LANES)))
        add("ssq", pl.BlockSpec((tm, LANES), lambda i, j: (i, 0)), ssq)
    if next_gain is not None:
        assert emit_stats
        add("next_gain", pl.BlockSpec((1, tn), lambda i, j: (0, j)), next_gain.reshape(1, n))
    if residual is not None:
        add("res", pl.BlockSpec((tm, tn), lambda i, j: (i, j)), residual)
    if side_cast is not None:
        src, src_layer = side_cast
        slab = src.shape[1] // ((m // tm) * nj)
        add("side_in", pl.BlockSpec((None, slab, src.shape[2]), lambda i, j: (src_layer, i * nj + j, 0)), src)
    names.append("out")
    out_specs = [pl.BlockSpec((tm, tn), lambda i, j: (i, j))]
    out_shape = [jax.ShapeDtypeStruct((m, n), out_dtype)]
    if side_cast is not None:
        names.append("side_out")
        out_specs.append(pl.BlockSpec((slab, src.shape[2]), lambda i, j: (i * nj + j, 0)))
        out_shape.append(jax.ShapeDtypeStruct(src.shape[1:], BF16))
    if emit_stats:
        assert out_dtype == F32
        names += ["xb", "ssq_out"]
        out_specs += [pl.BlockSpec((tm, tn), lambda i, j: (i, j)), pl.BlockSpec((tm, LANES), lambda i, j: (i, 0))]
        out_shape += [jax.ShapeDtypeStruct((m, n), BF16), jax.ShapeDtypeStruct((m, LANES), F32)]
    outs = pl.pallas_call(
        functools.partial(_matmul_kernel, names=tuple(names), act=act, norm_dim=kdim),
        grid=(m // tm, nj),
        in_specs=in_specs,
        out_specs=out_specs,
        out_shape=out_shape,
        compiler_params=_params("parallel", "arbitrary"),
        name=name,
    )(*args)
    return outs if len(outs) > 1 else outs[0]


def side_cast_fits(rows, m, n, tm, tn):
    return slab_rows(rows, (m // _tile(m, tm, SUBLANES)) * (n // _tile(n, tn))) is not None


def slab_rows(rows, steps):
    if steps <= 0 or rows % steps or (rows // steps) % BF16_SUBLANES:
        return None
    return rows // steps


def _matmul_ksplit_kernel(a_ref, w_ref, r_ref, o_ref, *stats_refs):
    k = pl.program_id(2)

    @pl.when(k == 0)
    def _():
        o_ref[...] = r_ref[...]

    o_ref[...] += jnp.dot(a_ref[...], w_ref[...], preferred_element_type=F32)
    if stats_refs:
        @pl.when(k == pl.num_programs(2) - 1)
        def _():
            _row_stats(o_ref[...], *stats_refs, pl.program_id(1) == 0)


def matmul_ksplit(a, w, residual, *, tm, tn, tk, emit_stats=False, name):
    m, kdim = a.shape
    n = w.shape[1]
    tm, tn, tk = _tile(m, tm, SUBLANES), _tile(n, tn), _tile(kdim, tk)
    nj = n // tn
    out_specs = [pl.BlockSpec((tm, tn), lambda i, j, k: (i, j))]
    out_shape = [jax.ShapeDtypeStruct((m, n), F32)]
    if emit_stats:
        out_specs += [pl.BlockSpec((tm, tn), lambda i, j, k: (i, j)),
                      pl.BlockSpec((tm, LANES), lambda i, j, k: (i, 0))]
        out_shape += [jax.ShapeDtypeStruct((m, n), BF16), jax.ShapeDtypeStruct((m, LANES), F32)]
    outs = pl.pallas_call(
        _matmul_ksplit_kernel,
        grid=(m // tm, nj, kdim // tk),
        in_specs=[pl.BlockSpec((tm, tk), lambda i, j, k: (i, k)),
                  pl.BlockSpec((tk, tn), lambda i, j, k: (k, j)),
                  pl.BlockSpec((tm, tn), lambda i, j, k: (i, j))],
        out_specs=out_specs,
        out_shape=out_shape,
        compiler_params=_params("parallel", "arbitrary", "arbitrary",
                                vmem_limit_bytes=KSPLIT_STATS_VMEM_BYTES if emit_stats else VMEM_LIMIT_BYTES),
        name=name,
    )(a, w, residual)
    return outs if emit_stats else outs[0]


CONV_ROWS = 64
CONV_HALO = 32
CONV_TS = 256
CONV_LANES = 256


def _sigmoid(x):
    return 0.5 * jnp.tanh(0.5 * x) + 0.5


def _conv_kernel(*refs, ts, seq, cw, lc, has_side):
    if has_side:
        a_ref, gate_ref, w_ref, cb_ref, lng_ref, lnb_ref, side_in, o_ref, side_out, buf_ref, c_ref = refs
        side_out[...] = side_in[...].astype(BF16)
    else:
        a_ref, gate_ref, w_ref, cb_ref, lng_ref, lnb_ref, o_ref, buf_ref, c_ref = refs
    i = pl.program_id(0)
    at_seq_start = (i * ts) % seq == 0

    @pl.when(at_seq_start)
    def _():
        buf_ref[0:CONV_HALO, :] = jnp.zeros((CONV_HALO, buf_ref.shape[1]), F32)

    @pl.when(jnp.logical_not(at_seq_start))
    def _():
        buf_ref[0:CONV_HALO, :] = buf_ref[ts:ts + CONV_HALO, :]

    buf_ref[CONV_HALO:CONV_HALO + ts, :] = a_ref[...] * _sigmoid(gate_ref[...])
    lead = CONV_HALO - (cw - 1)
    n_ch = a_ref.shape[1]

    def chunk(r, carry):
        r0 = pl.multiple_of(r * CONV_ROWS, CONV_ROWS)
        for c0 in range(0, n_ch, lc):
            cols = slice(c0, c0 + lc)
            acc = None
            for phase in range(SUBLANES):
                taps = [j for j in range(cw) if (lead + j) % SUBLANES == phase]
                if not taps:
                    continue
                rows = CONV_ROWS + (SUBLANES if phase else 0)
                part = None
                for j in taps:
                    base = (lead + j) // SUBLANES * SUBLANES
                    x = buf_ref[pl.ds(r0 + base, rows), cols].reshape(rows // SUBLANES, SUBLANES, lc)
                    term = x * w_ref[j, :, cols][None]
                    part = term if part is None else part + term
                part = part.reshape(rows, lc)
                piece = part[phase:phase + CONV_ROWS] if phase else part
                acc = piece if acc is None else acc + piece
            c_ref[pl.ds(r0, CONV_ROWS), cols] = acc + cb_ref[:, cols]
        c = c_ref[pl.ds(r0, CONV_ROWS), :]
        mu = jnp.mean(c, axis=-1, keepdims=True)
        cc = c - mu
        var = jnp.mean(cc * cc, axis=-1, keepdims=True)
        y = cc * lax.rsqrt(var + EPS) * lng_ref[...] + lnb_ref[...]
        o_ref[pl.ds(r0, CONV_ROWS), :] = (y * _sigmoid(y)).astype(o_ref.dtype)
        return carry

    lax.fori_loop(0, ts // CONV_ROWS, chunk, 0)


def conformer_conv(u, conv_w, conv_b, ln_g, ln_b, *, seq, side_cast=None):
    m = u.shape[0]
    cw, c = conv_w.shape
    assert cw - 1 <= CONV_HALO
    ts = _tile(seq, CONV_TS, CONV_HALO)
    lc = _tile(c, CONV_LANES)
    row = lambda v: v.reshape(1, c)
    in_specs = [pl.BlockSpec((ts, c), lambda i: (i, 0)),
                pl.BlockSpec((ts, c), lambda i: (i, 1)),
                pl.BlockSpec((cw, SUBLANES, c), lambda i: (0, 0, 0)),
                pl.BlockSpec((1, c), lambda i: (0, 0)),
                pl.BlockSpec((1, c), lambda i: (0, 0)),
                pl.BlockSpec((1, c), lambda i: (0, 0))]
    args = [u, u, jnp.broadcast_to(conv_w[:, None, :], (cw, SUBLANES, c)), row(conv_b), row(ln_g), row(ln_b)]
    out_specs = [pl.BlockSpec((ts, c), lambda i: (i, 0))]
    out_shape = [jax.ShapeDtypeStruct((m, c), BF16)]
    if side_cast is not None:
        src, src_layer = side_cast
        slab = slab_rows(src.shape[1], m // ts)
        in_specs.append(pl.BlockSpec((None, slab, src.shape[2]), lambda i: (src_layer, i, 0)))
        args.append(src)
        out_specs.append(pl.BlockSpec((slab, src.shape[2]), lambda i: (i, 0)))
        out_shape.append(jax.ShapeDtypeStruct(src.shape[1:], BF16))
    outs = pl.pallas_call(
        functools.partial(_conv_kernel, ts=ts, seq=seq, cw=cw, lc=lc, has_side=side_cast is not None),
        grid=(m // ts,),
        in_specs=in_specs,
        out_specs=out_specs,
        out_shape=out_shape,
        scratch_shapes=[pltpu.VMEM((CONV_HALO + ts, c), F32), pltpu.VMEM((ts, c), F32)],
        compiler_params=_params("arbitrary"),
        name="conformer_conv",
    )(*args)
    return outs if side_cast is not None else outs[0]


def _mem_kv_kernel(mem_ref, gn_ref, w_ref, gk_ref, o_ref, *, n_k_heads):
    j = pl.program_id(1)
    hn = (_rms(mem_ref[...]) * gn_ref[0]).astype(BF16)
    kv = jnp.dot(hn, w_ref[0].astype(BF16), preferred_element_type=F32)
    kn = _rms(kv) * gk_ref[0]
    o_ref[0] = jnp.where(j < n_k_heads, kn, kv).astype(o_ref.dtype)


def mem_kv(mem2d, norm_mem, w_mem_kv, g_mem_k):
    nl, d, n2 = w_mem_kv.shape
    hd = g_mem_k.shape[-1]
    bm = mem2d.shape[0]
    return pl.pallas_call(
        functools.partial(_mem_kv_kernel, n_k_heads=n2 // 2 // hd),
        grid=(nl, n2 // hd),
        in_specs=[pl.BlockSpec((bm, d), lambda l, j: (0, 0)),
                  pl.BlockSpec((1, 1, d), lambda l, j: (l, 0, 0)),
                  pl.BlockSpec((1, d, hd), lambda l, j: (l, 0, j)),
                  pl.BlockSpec((1, 1, hd), lambda l, j: (l, 0, 0))],
        out_specs=pl.BlockSpec((1, bm, hd), lambda l, j: (l, 0, j)),
        out_shape=jax.ShapeDtypeStruct((nl, bm, n2), BF16),
        compiler_params=_params("parallel", "parallel"),
        name="mem_kv",
    )(mem2d, norm_mem.reshape(nl, 1, d), w_mem_kv, g_mem_k.reshape(nl, 1, hd))


def _mem_attn_kernel(q_ref, k_ref, v_ref, gq_ref, o_ref, *, n_heads, hd):
    scale = hd ** -0.5
    for h in range(n_heads):
        cols = slice(h * hd, (h + 1) * hd)
        q = (_rms(q_ref[:, cols]) * (gq_ref[...] * scale)).astype(BF16)
        s = lax.dot_general(q, k_ref[0, :, cols], (((1,), (1,)), ((), ())), preferred_element_type=F32)
        p = jnp.exp(s - jnp.max(s, axis=-1, keepdims=True))
        denom = jnp.sum(p, axis=-1, keepdims=True)
        o = jnp.dot(p.astype(BF16), v_ref[0, :, cols], preferred_element_type=F32)
        o_ref[:, cols] = (o / denom).astype(o_ref.dtype)


def mem_attention(qsrc, q_col_block, kv, layer, g_q, *, seq, mem_len):
    m = qsrc.shape[0]
    w = kv.shape[-1] // 2
    hd = g_q.shape[-1]
    ts = _tile(seq, 512, SUBLANES)
    tiles_per_seq = seq // ts
    return pl.pallas_call(
        functools.partial(_mem_attn_kernel, n_heads=w // hd, hd=hd),
        grid=(m // ts,),
        in_specs=[pl.BlockSpec((ts, w), lambda i: (i, q_col_block)),
                  pl.BlockSpec((1, mem_len, w), lambda i: (layer, i // tiles_per_seq, 0)),
                  pl.BlockSpec((1, mem_len, w), lambda i: (layer, i // tiles_per_seq, 1)),
                  pl.BlockSpec((1, hd), lambda i: (0, 0))],
        out_specs=pl.BlockSpec((ts, w), lambda i: (i, 0)),
        out_shape=jax.ShapeDtypeStruct((m, w), BF16),
        compiler_params=_params("parallel"),
        name="mem_attention",
    )(qsrc, kv, kv, g_q.reshape(1, hd))


def _rope(t, cos_ref, sin_ref):
    partner = pltpu.roll(t, LANES // 4, 1)
    return t * cos_ref[...] + partner * sin_ref[...]


def _dkv_kernel(xb_ref, ssq_ref, gain_ref, w_ref, ga_ref, gr_ref, cos_ref, sin_ref, ckv_ref, kr_ref, *, lora, dim):
    g = gain_ref[...]
    w = w_ref[...]
    w = jnp.concatenate([w[:, c * LANES:(c + 1) * LANES] * g for c in range(w.shape[1] // LANES)], axis=1)
    ckr = jnp.dot(xb_ref[...], w.astype(BF16), preferred_element_type=F32)
    ckr = ckr * lax.rsqrt(jnp.sum(ssq_ref[...], axis=-1, keepdims=True) * (1.0 / dim) + EPS)
    ckv_ref[...] = (_rms(ckr[:, :lora]) * ga_ref[...]).astype(ckv_ref.dtype)
    kr = _rms(ckr[:, lora:]) * gr_ref[...]
    kr_ref[...] = _rope(kr, cos_ref, sin_ref).astype(kr_ref.dtype)


def kv_down(xb, ssq, gain, w_dkv_pad, g_a, g_kr_pad, cos_t, sin_t):
    m, d = xb.shape
    lora = g_a.shape[-1]
    tm = _tile(m, 1024, SUBLANES)
    full = lambda shape: pl.BlockSpec(shape, lambda i: (0, 0))
    rows = lambda width: pl.BlockSpec((tm, width), lambda i: (i, 0))
    return pl.pallas_call(
        functools.partial(_dkv_kernel, lora=lora, dim=d),
        grid=(m // tm,),
        in_specs=[rows(d), rows(LANES), full((d, LANES)), full(w_dkv_pad.shape), full((1, lora)), full((1, LANES)),
                  rows(LANES), rows(LANES)],
        out_specs=[rows(lora), rows(LANES)],
        out_shape=[jax.ShapeDtypeStruct((m, lora), BF16), jax.ShapeDtypeStruct((m, LANES), BF16)],
        compiler_params=_params("parallel"),
        name="kv_down",
    )(xb, ssq, jnp.broadcast_to(gain.reshape(d, 1), (d, LANES)), w_dkv_pad, g_a.reshape(1, lora), g_kr_pad,
      cos_t, sin_t)


HEAD_PAIR = 2


def _ukv_kernel(c_ref, wk_ref, wv_ref, kr_ref, gk_ref, k_ref, v_ref, *, heads, nope):
    c = c_ref[...]
    kr = kr_ref[...]
    width = HEAD_PAIR * nope
    for pair in range(heads // HEAD_PAIR):
        kn = jnp.dot(c, wk_ref[:, pair * width:(pair + 1) * width], preferred_element_type=F32)
        for h in range(HEAD_PAIR):
            base = (pair * HEAD_PAIR + h) * 2 * nope
            k_ref[:, base:base + nope] = (_rms(kn[:, h * nope:(h + 1) * nope]) * gk_ref[...]).astype(k_ref.dtype)
            k_ref[:, base + nope:base + 2 * nope] = kr
        cols = slice(pair * width, (pair + 1) * width)
        v_ref[:, cols] = jnp.dot(c, wv_ref[:, cols], preferred_element_type=F32).astype(v_ref.dtype)


def kv_up(c_kv, w_kn, w_v, kr, g_kn, *, n_heads):
    m, lora = c_kv.shape
    nope = g_kn.shape[-1]
    dv = w_v.shape[1] // n_heads
    assert nope == LANES and dv == LANES and n_heads % HEAD_PAIR == 0
    tm = _tile(m, 512, SUBLANES)
    whole = lambda shape: pl.BlockSpec(shape, lambda i: (0, 0))
    rows = lambda width: pl.BlockSpec((tm, width), lambda i: (i, 0))
    return pl.pallas_call(
        functools.partial(_ukv_kernel, heads=n_heads, nope=nope),
        grid=(m // tm,),
        in_specs=[rows(lora), whole(w_kn.shape), whole(w_v.shape), rows(LANES), whole((1, nope))],
        out_specs=[rows(n_heads * 2 * nope), rows(n_heads * dv)],
        out_shape=[jax.ShapeDtypeStruct((m, n_heads * 2 * nope), BF16),
                   jax.ShapeDtypeStruct((m, n_heads * dv), BF16)],
        compiler_params=_params("parallel"),
        name="kv_up",
    )(c_kv, w_kn, w_v, kr, g_kn.reshape(1, nope))


def _q_kernel(u_ref, ga_ref, w_ref, gn_ref, gr_ref, cos_ref, sin_ref, q_ref, cq_ref, *, heads, nope, scale):
    cq_ref[...] = (_rms(u_ref[...]) * ga_ref[...]).astype(cq_ref.dtype)
    width = HEAD_PAIR * 2 * nope
    for pair in range(heads // HEAD_PAIR):
        q = jnp.dot(cq_ref[...], w_ref[:, pair * width:(pair + 1) * width], preferred_element_type=F32)
        for h in range(HEAD_PAIR):
            base = h * 2 * nope
            out = pair * width + base
            qn = _rms(q[:, base:base + nope]) * (gn_ref[...] * scale)
            q_ref[:, out:out + nope] = qn.astype(q_ref.dtype)
            qr = _rms(q[:, base + nope:base + 2 * nope]) * (gr_ref[...] * scale)
            q_ref[:, out + nope:out + 2 * nope] = _rope(qr, cos_ref, sin_ref).astype(q_ref.dtype)


def q_proj(u, g_qa, w_q_pad, g_qn, g_qr_pad, cos_t, sin_t, *, n_heads, scale):
    m = u.shape[0]
    lora = g_qa.shape[-1]
    nope = g_qn.shape[-1]
    assert nope == LANES and n_heads % HEAD_PAIR == 0
    tm = _tile(m, 512, SUBLANES)
    whole = lambda shape: pl.BlockSpec(shape, lambda i: (0, 0))
    rows = lambda width: pl.BlockSpec((tm, width), lambda i: (i, 0))
    return pl.pallas_call(
        functools.partial(_q_kernel, heads=n_heads, nope=nope, scale=scale),
        grid=(m // tm,),
        in_specs=[rows(lora), whole((1, lora)), whole(w_q_pad.shape), whole((1, nope)), whole((1, LANES)),
                  rows(LANES), rows(LANES)],
        out_specs=rows(n_heads * 2 * nope),
        out_shape=jax.ShapeDtypeStruct((m, n_heads * 2 * nope), BF16),
        scratch_shapes=[pltpu.VMEM((tm, lora), BF16)],
        compiler_params=_params("parallel"),
        name="q_proj",
    )(u, g_qa.reshape(1, lora), w_q_pad, g_qn.reshape(1, nope), g_qr_pad, cos_t, sin_t)


ATT_TQ = 2048
ATT_TK = 512
ATT_GROUP = 4


def _mla_kernel(*refs, tq, tk, group, has_side):
    if has_side:
        q_ref, k_ref, v_ref, side_in, o_ref, side_out, m_ref, l_ref, acc_ref = refs
        side_out[...] = side_in[...].astype(BF16)
    else:
        q_ref, k_ref, v_ref, o_ref, m_ref, l_ref, acc_ref = refs
    i = pl.program_id(2)
    m_ref[...] = jnp.full(m_ref.shape, -jnp.inf, F32)
    l_ref[...] = jnp.zeros(l_ref.shape, F32)
    acc_ref[...] = jnp.zeros(acc_ref.shape, F32)
    chunks = tk // LANES

    def sub_block(start, diag_offset=None):
        rows = slice(0 if diag_offset is None else diag_offset, tq)
        s = lax.dot_general(q_ref[rows, :], k_ref[pl.ds(start, tk), :], (((1,), (1,)), ((), ())),
                            preferred_element_type=F32)
        if diag_offset is not None:
            row = lax.broadcasted_iota(jnp.int32, s.shape, 0)
            col = lax.broadcasted_iota(jnp.int32, s.shape, 1)
            s = jnp.where(col <= row, s, -jnp.inf)
        m_prev = m_ref[rows, :]
        m_new = jnp.maximum(m_prev, jnp.max(s, axis=-1, keepdims=True))
        alpha = jnp.exp2(m_prev - m_new)
        p = [jnp.exp2(s[:, c * LANES:(c + 1) * LANES] - m_new) for c in range(chunks)]
        l_ref[rows, :] = alpha * l_ref[rows, :] + functools.reduce(lambda a, b: a + b, p)
        pv = jnp.dot(jnp.concatenate(p, axis=1).astype(BF16), v_ref[pl.ds(start, tk), :],
                     preferred_element_type=F32)
        acc_ref[rows, :] = alpha * acc_ref[rows, :] + pv
        m_ref[rows, :] = m_new

    per_tile = tq // tk
    n_off = i * per_tile
    n_groups = n_off // group

    def body(g, carry):
        for t in range(group):
            sub_block(pl.multiple_of((g * group + t) * tk, tk))
        return carry

    lax.fori_loop(0, n_groups, body, 0)
    step = per_tile
    while group % step:
        step -= 1
    for rem in range(step, group, step):
        @pl.when(n_off % group == rem)
        def _(rem=rem):
            for t in range(rem):
                sub_block(pl.multiple_of((n_groups * group + t) * tk, tk))
    for t in range(per_tile):
        sub_block(pl.multiple_of(i * tq + t * tk, tk), diag_offset=t * tk)
    denom = jnp.sum(l_ref[...], axis=-1, keepdims=True)
    o_ref[...] = (acc_ref[...] / denom).astype(o_ref.dtype)


def attention_cast_heads(rows, batch, seq, n_heads):
    nq = seq // _tile(seq, ATT_TQ, ATT_TK)
    for heads in range(n_heads, 0, -1):
        if slab_rows(rows, batch * heads * nq) is not None:
            return heads
    return 0


def mla_attention(q, k, v, *, batch, seq, n_heads, side_cast=None):
    m = q.shape[0]
    dqk = q.shape[1] // n_heads
    dv = v.shape[1] // n_heads
    assert dv == LANES
    tq = _tile(seq, ATT_TQ, ATT_TK)
    tk = min(ATT_TK, tq)
    nq = seq // tq
    in_specs = [pl.BlockSpec((tq, dqk), lambda b, h, i: (b * nq + i, h)),
                pl.BlockSpec((seq, dqk), lambda b, h, i: (b, h)),
                pl.BlockSpec((seq, dv), lambda b, h, i: (b, h))]
    args = [q, k, v]
    out_specs = [pl.BlockSpec((tq, dv), lambda b, h, i: (b * nq + i, h))]
    out_shape = [jax.ShapeDtypeStruct((m, n_heads * dv), BF16)]
    if side_cast is not None:
        src, src_layer = side_cast
        heads = attention_cast_heads(src.shape[1], batch, seq, n_heads)
        slab = slab_rows(src.shape[1], batch * heads * nq)

        def slab_index(b, h, i):
            done = h >= heads
            return (b * heads + jnp.minimum(h, heads - 1)) * nq + jnp.where(done, nq - 1, i)

        in_specs.append(pl.BlockSpec((None, slab, src.shape[2]), lambda b, h, i: (src_layer, slab_index(b, h, i), 0)))
        args.append(src)
        out_specs.append(pl.BlockSpec((slab, src.shape[2]), lambda b, h, i: (slab_index(b, h, i), 0)))
        out_shape.append(jax.ShapeDtypeStruct(src.shape[1:], BF16))
    outs = pl.pallas_call(
        functools.partial(_mla_kernel, tq=tq, tk=tk, group=ATT_GROUP, has_side=side_cast is not None),
        grid=(batch, n_heads, nq),
        in_specs=in_specs,
        out_specs=out_specs,
        out_shape=out_shape,
        scratch_shapes=[pltpu.VMEM((tq, LANES), F32), pltpu.VMEM((tq, LANES), F32), pltpu.VMEM((tq, dv), F32)],
        compiler_params=_params("parallel", "arbitrary", "arbitrary"),
        name="mla_attention",
    )(*args)
    return outs if side_cast is not None else outs[0]


WIDE_TM, WIDE_TN = 2048, 512
OUT_PROJ_TM, OUT_PROJ_TN = 1024, 512
B_IN_TM, B_IN_TN = 1024, 512
MLP_IN_TM, MLP_IN_TN = 1024, 1024
MLP_OUT_TM, MLP_OUT_TN, MLP_OUT_TK = 1024, 1024, 4096


def _wide(a_parts, w_f32, **kw):
    return matmul(a_parts, w_f32, tm=WIDE_TM, tn=WIDE_TN, single_buffer_a=True, **kw)


def _mlp(x, xg, ssq, w_in_bf16, w_out, layer, emit_stats):
    kw = dict(norm=(None, ssq), out_dtype=BF16, act="relu2", tm=MLP_IN_TM, tn=MLP_IN_TN, name="mlp_in")
    if side_cast_fits(w_out.shape[1], x.shape[0], w_in_bf16.shape[-1], MLP_IN_TM, MLP_IN_TN):
        hm, w_out_bf16 = matmul([xg], w_in_bf16, side_cast=(w_out, layer), **kw)
    else:
        hm, w_out_bf16 = matmul([xg], w_in_bf16, **kw), w_out[layer].astype(BF16)
    return matmul_ksplit(hm, w_out_bf16, x, tm=MLP_OUT_TM, tn=MLP_OUT_TN, tk=MLP_OUT_TK, emit_stats=emit_stats,
                         name="mlp_out")


def kernel(x, mem, positions, norm_mix, norm_mlp, norm_mem, w_mem_kv, g_mem_q, g_mem_k, w_mlp_in, w_mlp_out,
           a_w_in, a_conv_w, a_conv_b, a_ln_g, a_ln_b, a_w_out, b_w_in, b_g_qa, b_w_uq, b_g_qn, b_g_qr, b_w_out,
           kv_g_in, kv_w_dkv, kv_g_a, kv_w_ukv, kv_g_kn, kv_g_kr):
    batch, seq, d = x.shape
    m = batch * seq
    mem_len = mem.shape[1]
    mem_w = w_mem_kv.shape[-1] // 2
    conv_ch = a_conv_w.shape[-1]
    q_lora = b_g_qa.shape[-1]
    kv_lora = kv_g_a.shape[-1]
    nope, rope = b_g_qn.shape[-1], b_g_qr.shape[-1]
    n_heads = b_w_uq.shape[-1] // (nope + rope)
    dv = kv_w_ukv.shape[-1] // n_heads - nope
    assert 2 * rope == LANES and nope == LANES and dv == LANES
    assert (2 * conv_ch) % mem_w == 0 and q_lora % mem_w == 0
    assert b_w_in.shape[0] == 1 and a_w_in.shape[0] == 1 and norm_mix.shape[0] == 2

    cast = lambda w: w.astype(BF16)
    w_uq = b_w_uq[0].reshape(q_lora, n_heads, nope + rope)
    w_q_pad = cast(jnp.concatenate([w_uq, w_uq[:, :, nope:]], axis=-1).reshape(q_lora, n_heads * 2 * nope))
    w_ukv = kv_w_ukv.reshape(kv_lora, n_heads, nope + dv)
    w_kn = cast(w_ukv[:, :, :nope].reshape(kv_lora, n_heads * nope))
    w_v = cast(w_ukv[:, :, nope:].reshape(kv_lora, n_heads * dv))
    w_dkv_pad = jnp.concatenate([kv_w_dkv, kv_w_dkv[:, kv_lora:]], axis=-1)
    pad_gain = lambda g: jnp.concatenate([g, g]).reshape(1, 2 * rope)

    inv_freq = ROPE_THETA ** (-jnp.arange(0, rope, 2, dtype=F32) / rope)
    ang = positions.astype(F32).reshape(m, 1) * inv_freq
    zeros = jnp.zeros((m, LANES - rope), F32)
    cos_t = jnp.concatenate([jnp.cos(ang), jnp.cos(ang), zeros], axis=-1)
    sin_t = jnp.concatenate([-jnp.sin(ang), jnp.sin(ang), zeros], axis=-1)

    x = x.reshape(m, d)
    kv_mem = mem_kv(mem.reshape(batch * mem_len, d), norm_mem, w_mem_kv, g_mem_k)

    out_proj = functools.partial(matmul, out_dtype=F32, tm=OUT_PROJ_TM, tn=OUT_PROJ_TN, emit_stats=True)
    h, = rmsnorm_cast(x, norm_mix[0:1])
    u = _wide([h], a_w_in[0], out_dtype=F32, name="a_in_proj")
    conv_args = (u, a_conv_w[0], a_conv_b[0], a_ln_g[0], a_ln_b[0])
    if slab_rows(w_mlp_in.shape[1], m // _tile(seq, CONV_TS, CONV_HALO)) is not None:
        y_main, w_in0 = conformer_conv(*conv_args, seq=seq, side_cast=(w_mlp_in, 0))
    else:
        y_main, w_in0 = conformer_conv(*conv_args, seq=seq), cast(w_mlp_in[0])
    y_mem = mem_attention(u, 2 * conv_ch // mem_w, kv_mem, 0, g_mem_q[0], seq=seq, mem_len=mem_len)
    x, xg, ssq = out_proj([y_main, y_mem], cast(a_w_out[0]), residual=x, next_gain=norm_mlp[0], name="a_out_proj")
    x, xb, ssq = _mlp(x, xg, ssq, w_in0, w_mlp_out, 0, emit_stats=True)

    c_kv, k_rope = kv_down(xb, ssq, kv_g_in, w_dkv_pad, kv_g_a, pad_gain(kv_g_kr), cos_t, sin_t)
    k_all, v_all = kv_up(c_kv, w_kn, w_v, k_rope, kv_g_kn, n_heads=n_heads)
    u = matmul([xb], b_w_in[0], norm=(norm_mix[1], ssq), out_dtype=F32, tm=B_IN_TM, tn=B_IN_TN, name="b_in_proj")
    q_all = q_proj(u, b_g_qa[0], w_q_pad, b_g_qn[0], pad_gain(b_g_qr[0]), cos_t, sin_t,
                   n_heads=n_heads, scale=(nope + rope) ** -0.5 * math.log2(math.e))
    if attention_cast_heads(w_mlp_in.shape[1], batch, seq, n_heads):
        y_main, w_in1 = mla_attention(q_all, k_all, v_all, batch=batch, seq=seq, n_heads=n_heads,
                                      side_cast=(w_mlp_in, 1))
    else:
        y_main = mla_attention(q_all, k_all, v_all, batch=batch, seq=seq, n_heads=n_heads)
        w_in1 = cast(w_mlp_in[1])
    y_mem = mem_attention(u, q_lora // mem_w, kv_mem, 1, g_mem_q[1], seq=seq, mem_len=mem_len)
    x, xg, ssq = out_proj([y_main, y_mem], cast(b_w_out[0]), residual=x, next_gain=norm_mlp[1], name="b_out_proj")
    x = _mlp(x, xg, ssq, w_in1, w_mlp_out, 1, emit_stats=False)
    return x.reshape(batch, seq, d)
```

```python
import functools
import math

import jax
import jax.numpy as jnp
from jax import lax
from jax.experimental import pallas as pl
from jax.experimental.pallas import tpu as pltpu

EPS = 1e-6
ROPE_THETA = 10000.0
LANES = 128
SUBLANES = 8
BF16_SUBLANES = 16
MIB = 1 << 20
VMEM_LIMIT_BYTES = 56 * MIB
KSPLIT_STATS_VMEM_BYTES = 62 * MIB
F32 = jnp.float32
BF16 = jnp.bfloat16


def _tile(n, pref, mult=LANES):
    if n <= pref:
        return n
    t = (pref // mult) * mult
    while t >= mult:
        if n % t == 0:
            return t
        t -= mult
    raise ValueError(f"no tile for {n} (pref {pref}, mult {mult})")


def _params(*semantics, vmem_limit_bytes=VMEM_LIMIT_BYTES):
    return pltpu.CompilerParams(dimension_semantics=semantics, vmem_limit_bytes=vmem_limit_bytes)


def _rms(x):
    ms = jnp.sum(x * x, axis=-1, keepdims=True) * (1.0 / x.shape[-1])
    return x * lax.rsqrt(ms + EPS)


def _rmsnorm_kernel(x_ref, g_ref, *o_refs):
    xn = _rms(x_ref[...])
    for i, o_ref in enumerate(o_refs):
        o_ref[...] = (xn * g_ref[i:i + 1, :]).astype(o_ref.dtype)


def rmsnorm_cast(x, gains):
    m, d = x.shape
    n = gains.shape[0]
    tm = _tile(m, 512, SUBLANES)
    return pl.pallas_call(
        _rmsnorm_kernel,
        grid=(m // tm,),
        in_specs=[pl.BlockSpec((tm, d), lambda i: (i, 0)),
                  pl.BlockSpec((n, d), lambda i: (0, 0))],
        out_specs=[pl.BlockSpec((tm, d), lambda i: (i, 0)) for _ in range(n)],
        out_shape=[jax.ShapeDtypeStruct((m, d), BF16) for _ in range(n)],
        compiler_params=_params("parallel"),
        name="rmsnorm_cast",
    )(x, gains)


def _scale_cast_kernel(w_ref, g_ref, o_ref):
    g = g_ref[...]
    for c in range(w_ref.shape[1] // LANES):
        cols = slice(c * LANES, (c + 1) * LANES)
        o_ref[:, cols] = (w_ref[:, cols] * g).astype(o_ref.dtype)


def scale_rows_cast(w, gain):
    k, n = w.shape
    tk = _tile(k, 512, SUBLANES)
    return pl.pallas_call(
        _scale_cast_kernel,
        grid=(k // tk,),
        in_specs=[pl.BlockSpec((tk, n), lambda i: (i, 0)), pl.BlockSpec((tk, LANES), lambda i: (i, 0))],
        out_specs=pl.BlockSpec((tk, n), lambda i: (i, 0)),
        out_shape=jax.ShapeDtypeStruct((k, n), BF16),
        compiler_params=_params("parallel"),
        name="scale_rows_cast",
    )(w, jnp.broadcast_to(gain.reshape(k, 1), (k, LANES)))


def _row_stats(x, xb_ref, ssq_ref, first_col_tile, gain=None):
    xb_ref[...] = (x if gain is None else x * gain).astype(xb_ref.dtype)
    sq = x * x
    partial = functools.reduce(lambda p, q: p + q,
                               [sq[:, c * LANES:(c + 1) * LANES] for c in range(x.shape[1] // LANES)])

    @pl.when(first_col_tile)
    def _():
        ssq_ref[...] = partial

    @pl.when(jnp.logical_not(first_col_tile))
    def _():
        ssq_ref[...] += partial


def _matmul_kernel(*refs, names, act, norm_dim):
    r = dict(zip(names, refs))
    if "side_in" in r:
        r["side_out"][...] = r["side_in"][...].astype(BF16)
    acc = None
    for key in names:
        if not key.startswith("a"):
            continue
        part = jnp.dot(r[key][...], r["w" + key[1:]][...].astype(BF16), preferred_element_type=F32)
        acc = part if acc is None else acc + part
    if "ssq" in r:
        acc = acc * lax.rsqrt(jnp.sum(r["ssq"][...], axis=-1, keepdims=True) * (1.0 / norm_dim) + EPS)
    if act == "relu2":
        acc = jnp.square(jnp.maximum(acc, 0.0))
    if "res" in r:
        acc = acc + r["res"][...]
    r["out"][...] = acc.astype(r["out"].dtype)
    if "xb" in r:
        _row_stats(acc, r["xb"], r["ssq_out"], pl.program_id(1) == 0,
                   gain=r["next_gain"][...] if "next_gain" in r else None)


def matmul(a_parts, w, *, out_dtype, act=None, residual=None, tm, tn, single_buffer_a=False,
           side_cast=None, norm_ssq=None, emit_stats=False, next_gain=None, name):
    m = a_parts[0].shape[0]
    n = w.shape[-1]
    tm, tn = _tile(m, tm, SUBLANES), _tile(n, tn)
    nj = n // tn
    a_mode = dict(pipeline_mode=pl.Buffered(1)) if single_buffer_a else {}
    names, in_specs, args = [], [], []

    def add(name_, spec, arr):
        names.append(name_)
        in_specs.append(spec)
        args.append(arr)

    offset = 0
    for p, a in enumerate(a_parts):
        kp = a.shape[1]
        assert offset % kp == 0
        add(f"a{p}", pl.BlockSpec((tm, kp), lambda i, j: (i, 0), **a_mode), a)
        add(f"w{p}", pl.BlockSpec((kp, tn), lambda i, j, blk=offset // kp: (blk, j)), w)
        offset += kp
    kdim = offset
    assert kdim == w.shape[-2]
    if norm_ssq is not None:
        assert len(a_parts) == 1
        add("ssq", pl.BlockSpec((tm, LANES), lambda i, j: (i, 0)), norm_ssq)
    if next_gain is not None:
        assert emit_stats
        add("next_gain", pl.BlockSpec((1, tn), lambda i, j: (0, j)), next_gain.reshape(1, n))
    if residual is not None:
        add("res", pl.BlockSpec((tm, tn), lambda i, j: (i, j)), residual)
    if side_cast is not None:
        src, src_layer = side_cast
        slab = src.shape[1] // ((m // tm) * nj)
        add("side_in", pl.BlockSpec((None, slab, src.shape[2]), lambda i, j: (src_layer, i * nj + j, 0)), src)
    names.append("out")
    out_specs = [pl.BlockSpec((tm, tn), lambda i, j: (i, j))]
    out_shape = [jax.ShapeDtypeStruct((m, n), out_dtype)]
    if side_cast is not None:
        names.append("side_out")
        out_specs.append(pl.BlockSpec((slab, src.shape[2]), lambda i, j: (i * nj + j, 0)))
        out_shape.append(jax.ShapeDtypeStruct(src.shape[1:], BF16))
    if emit_stats:
        assert out_dtype == F32
        names += ["xb", "ssq_out"]
        out_specs += [pl.BlockSpec((tm, tn), lambda i, j: (i, j)), pl.BlockSpec((tm, LANES), lambda i, j: (i, 0))]
        out_shape += [jax.ShapeDtypeStruct((m, n), BF16), jax.ShapeDtypeStruct((m, LANES), F32)]
    outs = pl.pallas_call(
        functools.partial(_matmul_kernel, names=tuple(names), act=act, norm_dim=kdim),
        grid=(m // tm, nj),
        in_specs=in_specs,
        out_specs=out_specs,
        out_shape=out_shape,
        compiler_params=_params("parallel", "arbitrary"),
        name=name,
    )(*args)
    return outs if len(outs) > 1 else outs[0]


def side_cast_fits(rows, m, n, tm, tn):
    return slab_rows(rows, (m // _tile(m, tm, SUBLANES)) * (n // _tile(n, tn))) is not None


def slab_rows(rows, steps):
    if steps <= 0 or rows % steps or (rows // steps) % BF16_SUBLANES:
        return None
    return rows // steps


def _matmul_ksplit_kernel(a_ref, w_ref, r_ref, o_ref, *stats_refs):
    k = pl.program_id(2)

    @pl.when(k == 0)
    def _():
        o_ref[...] = r_ref[...]

    o_ref[...] += jnp.dot(a_ref[...], w_ref[...], preferred_element_type=F32)
    if stats_refs:
        @pl.when(k == pl.num_programs(2) - 1)
        def _():
            _row_stats(o_ref[...], *stats_refs, pl.program_id(1) == 0)


def matmul_ksplit(a, w, residual, *, tm, tn, tk, emit_stats=False, name):
    m, kdim = a.shape
    n = w.shape[1]
    tm, tn, tk = _tile(m, tm, SUBLANES), _tile(n, tn), _tile(kdim, tk)
    nj = n // tn
    out_specs = [pl.BlockSpec((tm, tn), lambda i, j, k: (i, j))]
    out_shape = [jax.ShapeDtypeStruct((m, n), F32)]
    if emit_stats:
        out_specs += [pl.BlockSpec((tm, tn), lambda i, j, k: (i, j)),
                      pl.BlockSpec((tm, LANES), lambda i, j, k: (i, 0))]
        out_shape += [jax.ShapeDtypeStruct((m, n), BF16), jax.ShapeDtypeStruct((m, LANES), F32)]
    outs = pl.pallas_call(
        _matmul_ksplit_kernel,
        grid=(m // tm, nj, kdim // tk),
        in_specs=[pl.BlockSpec((tm, tk), lambda i, j, k: (i, k)),
                  pl.BlockSpec((tk, tn), lambda i, j, k: (k, j)),
                  pl.BlockSpec((tm, tn), lambda i, j, k: (i, j))],
        out_specs=out_specs,
        out_shape=out_shape,
        compiler_params=_params("parallel", "arbitrary", "arbitrary",
                                vmem_limit_bytes=KSPLIT_STATS_VMEM_BYTES if emit_stats else VMEM_LIMIT_BYTES),
        name=name,
    )(a, w, residual)
    return outs if emit_stats else outs[0]


CONV_ROWS = 64
CONV_HALO = 32
CONV_TS = 256
CONV_LANES = 256


def _sigmoid(x):
    return 0.5 * jnp.tanh(0.5 * x) + 0.5


def _conv_kernel(*refs, ts, seq, cw, lc, has_side):
    if has_side:
        a_ref, gate_ref, w_ref, cb_ref, lng_ref, lnb_ref, side_in, o_ref, side_out, buf_ref, c_ref = refs
        side_out[...] = side_in[...].astype(BF16)
    else:
        a_ref, gate_ref, w_ref, cb_ref, lng_ref, lnb_ref, o_ref, buf_ref, c_ref = refs
    i = pl.program_id(0)
    at_seq_start = (i * ts) % seq == 0

    @pl.when(at_seq_start)
    def _():
        buf_ref[0:CONV_HALO, :] = jnp.zeros((CONV_HALO, buf_ref.shape[1]), F32)

    @pl.when(jnp.logical_not(at_seq_start))
    def _():
        buf_ref[0:CONV_HALO, :] = buf_ref[ts:ts + CONV_HALO, :]

    buf_ref[CONV_HALO:CONV_HALO + ts, :] = a_ref[...] * _sigmoid(gate_ref[...])
    lead = CONV_HALO - (cw - 1)
    n_ch = a_ref.shape[1]

    def chunk(r, carry):
        r0 = pl.multiple_of(r * CONV_ROWS, CONV_ROWS)
        for c0 in range(0, n_ch, lc):
            cols = slice(c0, c0 + lc)
            acc = None
            for phase in range(SUBLANES):
                taps = [j for j in range(cw) if (lead + j) % SUBLANES == phase]
                if not taps:
                    continue
                rows = CONV_ROWS + (SUBLANES if phase else 0)
                part = None
                for j in taps:
                    base = (lead + j) // SUBLANES * SUBLANES
                    x = buf_ref[pl.ds(r0 + base, rows), cols].reshape(rows // SUBLANES, SUBLANES, lc)
                    term = x * w_ref[j, :, cols][None]
                    part = term if part is None else part + term
                part = part.reshape(rows, lc)
                piece = part[phase:phase + CONV_ROWS] if phase else part
                acc = piece if acc is None else acc + piece
            c_ref[pl.ds(r0, CONV_ROWS), cols] = acc + cb_ref[:, cols]
        c = c_ref[pl.ds(r0, CONV_ROWS), :]
        mu = jnp.mean(c, axis=-1, keepdims=True)
        cc = c - mu
        var = jnp.mean(cc * cc, axis=-1, keepdims=True)
        y = cc * lax.rsqrt(var + EPS) * lng_ref[...] + lnb_ref[...]
        o_ref[pl.ds(r0, CONV_ROWS), :] = (y * _sigmoid(y)).astype(o_ref.dtype)
        return carry

    lax.fori_loop(0, ts // CONV_ROWS, chunk, 0)


def conformer_conv(u, conv_w, conv_b, ln_g, ln_b, *, seq, side_cast=None):
    m = u.shape[0]
    cw, c = conv_w.shape
    assert cw - 1 <= CONV_HALO
    ts = _tile(seq, CONV_TS, CONV_HALO)
    lc = _tile(c, CONV_LANES)
    row = lambda v: v.reshape(1, c)
    in_specs = [pl.BlockSpec((ts, c), lambda i: (i, 0)),
                pl.BlockSpec((ts, c), lambda i: (i, 1)),
                pl.BlockSpec((cw, SUBLANES, c), lambda i: (0, 0, 0)),
                pl.BlockSpec((1, c), lambda i: (0, 0)),
                pl.BlockSpec((1, c), lambda i: (0, 0)),
                pl.BlockSpec((1, c), lambda i: (0, 0))]
    args = [u, u, jnp.broadcast_to(conv_w[:, None, :], (cw, SUBLANES, c)), row(conv_b), row(ln_g), row(ln_b)]
    out_specs = [pl.BlockSpec((ts, c), lambda i: (i, 0))]
    out_shape = [jax.ShapeDtypeStruct((m, c), BF16)]
    if side_cast is not None:
        src, src_layer = side_cast
        slab = slab_rows(src.shape[1], m // ts)
        in_specs.append(pl.BlockSpec((None, slab, src.shape[2]), lambda i: (src_layer, i, 0)))
        args.append(src)
        out_specs.append(pl.BlockSpec((slab, src.shape[2]), lambda i: (i, 0)))
        out_shape.append(jax.ShapeDtypeStruct(src.shape[1:], BF16))
    outs = pl.pallas_call(
        functools.partial(_conv_kernel, ts=ts, seq=seq, cw=cw, lc=lc, has_side=side_cast is not None),
        grid=(m // ts,),
        in_specs=in_specs,
        out_specs=out_specs,
        out_shape=out_shape,
        scratch_shapes=[pltpu.VMEM((CONV_HALO + ts, c), F32), pltpu.VMEM((ts, c), F32)],
        compiler_params=_params("arbitrary"),
        name="conformer_conv",
    )(*args)
    return outs if side_cast is not None else outs[0]


def _mem_kv_kernel(mem_ref, gn_ref, w_ref, gk_ref, o_ref, *, n_k_heads):
    j = pl.program_id(1)
    hn = (_rms(mem_ref[...]) * gn_ref[0]).astype(BF16)
    kv = jnp.dot(hn, w_ref[0].astype(BF16), preferred_element_type=F32)
    kn = _rms(kv) * gk_ref[0]
    o_ref[0] = jnp.where(j < n_k_heads, kn, kv).astype(o_ref.dtype)


def mem_kv(mem2d, norm_mem, w_mem_kv, g_mem_k):
    nl, d, n2 = w_mem_kv.shape
    hd = g_mem_k.shape[-1]
    bm = mem2d.shape[0]
    return pl.pallas_call(
        functools.partial(_mem_kv_kernel, n_k_heads=n2 // 2 // hd),
        grid=(nl, n2 // hd),
        in_specs=[pl.BlockSpec((bm, d), lambda l, j: (0, 0)),
                  pl.BlockSpec((1, 1, d), lambda l, j: (l, 0, 0)),
                  pl.BlockSpec((1, d, hd), lambda l, j: (l, 0, j)),
                  pl.BlockSpec((1, 1, hd), lambda l, j: (l, 0, 0))],
        out_specs=pl.BlockSpec((1, bm, hd), lambda l, j: (l, 0, j)),
        out_shape=jax.ShapeDtypeStruct((nl, bm, n2), BF16),
        compiler_params=_params("parallel", "parallel"),
        name="mem_kv",
    )(mem2d, norm_mem.reshape(nl, 1, d), w_mem_kv, g_mem_k.reshape(nl, 1, hd))


def _mem_attn_kernel(q_ref, k_ref, v_ref, gq_ref, o_ref, *, n_heads, hd):
    scale = hd ** -0.5
    for h in range(n_heads):
        cols = slice(h * hd, (h + 1) * hd)
        q = (_rms(q_ref[:, cols]) * (gq_ref[...] * scale)).astype(BF16)
        s = lax.dot_general(q, k_ref[0, :, cols], (((1,), (1,)), ((), ())), preferred_element_type=F32)
        p = jnp.exp(s - jnp.max(s, axis=-1, keepdims=True))
        denom = jnp.sum(p, axis=-1, keepdims=True)
        o = jnp.dot(p.astype(BF16), v_ref[0, :, cols], preferred_element_type=F32)
        o_ref[:, cols] = (o / denom).astype(o_ref.dtype)


def mem_attention(qsrc, q_col_block, kv, layer, g_q, *, seq, mem_len):
    m = qsrc.shape[0]
    w = kv.shape[-1] // 2
    hd = g_q.shape[-1]
    ts = _tile(seq, 512, SUBLANES)
    tiles_per_seq = seq // ts
    return pl.pallas_call(
        functools.partial(_mem_attn_kernel, n_heads=w // hd, hd=hd),
        grid=(m // ts,),
        in_specs=[pl.BlockSpec((ts, w), lambda i: (i, q_col_block)),
                  pl.BlockSpec((1, mem_len, w), lambda i: (layer, i // tiles_per_seq, 0)),
                  pl.BlockSpec((1, mem_len, w), lambda i: (layer, i // tiles_per_seq, 1)),
                  pl.BlockSpec((1, hd), lambda i: (0, 0))],
        out_specs=pl.BlockSpec((ts, w), lambda i: (i, 0)),
        out_shape=jax.ShapeDtypeStruct((m, w), BF16),
        compiler_params=_params("parallel"),
        name="mem_attention",
    )(qsrc, kv, kv, g_q.reshape(1, hd))


def _rope(t, cos_ref, sin_ref):
    partner = pltpu.roll(t, LANES // 4, 1)
    return t * cos_ref[...] + partner * sin_ref[...]


def _dkv_kernel(xb_ref, ssq_ref, w_ref, ga_ref, gr_ref, cos_ref, sin_ref, ckv_ref, kr_ref, *, lora, dim):
    ckr = jnp.dot(xb_ref[...], w_ref[...], preferred_element_type=F32)
    ckr = ckr * lax.rsqrt(jnp.sum(ssq_ref[...], axis=-1, keepdims=True) * (1.0 / dim) + EPS)
    ckv_ref[...] = (_rms(ckr[:, :lora]) * ga_ref[...]).astype(ckv_ref.dtype)
    kr = _rms(ckr[:, lora:]) * gr_ref[...]
    kr_ref[...] = _rope(kr, cos_ref, sin_ref).astype(kr_ref.dtype)


def kv_down(xb, ssq, w_dkv_pad, g_a, g_kr_pad, cos_t, sin_t):
    m, d = xb.shape
    lora = g_a.shape[-1]
    tm = _tile(m, 1024, SUBLANES)
    full = lambda shape: pl.BlockSpec(shape, lambda i: (0, 0))
    rows = lambda width: pl.BlockSpec((tm, width), lambda i: (i, 0))
    return pl.pallas_call(
        functools.partial(_dkv_kernel, lora=lora, dim=d),
        grid=(m // tm,),
        in_specs=[rows(d), rows(LANES), full(w_dkv_pad.shape), full((1, lora)), full((1, LANES)),
                  rows(LANES), rows(LANES)],
        out_specs=[rows(lora), rows(LANES)],
        out_shape=[jax.ShapeDtypeStruct((m, lora), BF16), jax.ShapeDtypeStruct((m, LANES), BF16)],
        compiler_params=_params("parallel"),
        name="kv_down",
    )(xb, ssq, w_dkv_pad, g_a.reshape(1, lora), g_kr_pad, cos_t, sin_t)


HEAD_PAIR = 2


def _ukv_kernel(c_ref, wk_ref, wv_ref, kr_ref, gk_ref, k_ref, v_ref, *, heads, nope):
    c = c_ref[...]
    kr = kr_ref[...]
    width = HEAD_PAIR * nope
    for pair in range(heads // HEAD_PAIR):
        kn = jnp.dot(c, wk_ref[:, pair * width:(pair + 1) * width], preferred_element_type=F32)
        for h in range(HEAD_PAIR):
            base = (pair * HEAD_PAIR + h) * 2 * nope
            k_ref[:, base:base + nope] = (_rms(kn[:, h * nope:(h + 1) * nope]) * gk_ref[...]).astype(k_ref.dtype)
            k_ref[:, base + nope:base + 2 * nope] = kr
        cols = slice(pair * width, (pair + 1) * width)
        v_ref[:, cols] = jnp.dot(c, wv_ref[:, cols], preferred_element_type=F32).astype(v_ref.dtype)


def kv_up(c_kv, w_kn, w_v, kr, g_kn, *, n_heads):
    m, lora = c_kv.shape
    nope = g_kn.shape[-1]
    dv = w_v.shape[1] // n_heads
    assert nope == LANES and dv == LANES and n_heads % HEAD_PAIR == 0
    tm = _tile(m, 512, SUBLANES)
    whole = lambda shape: pl.BlockSpec(shape, lambda i: (0, 0))
    rows = lambda width: pl.BlockSpec((tm, width), lambda i: (i, 0))
    return pl.pallas_call(
        functools.partial(_ukv_kernel, heads=n_heads, nope=nope),
        grid=(m // tm,),
        in_specs=[rows(lora), whole(w_kn.shape), whole(w_v.shape), rows(LANES), whole((1, nope))],
        out_specs=[rows(n_heads * 2 * nope), rows(n_heads * dv)],
        out_shape=[jax.ShapeDtypeStruct((m, n_heads * 2 * nope), BF16),
                   jax.ShapeDtypeStruct((m, n_heads * dv), BF16)],
        compiler_params=_params("parallel"),
        name="kv_up",
    )(c_kv, w_kn, w_v, kr, g_kn.reshape(1, nope))


def _q_kernel(u_ref, ga_ref, w_ref, gn_ref, gr_ref, cos_ref, sin_ref, q_ref, cq_ref, *, heads, nope, scale):
    cq_ref[...] = (_rms(u_ref[...]) * ga_ref[...]).astype(cq_ref.dtype)
    width = HEAD_PAIR * 2 * nope
    for pair in range(heads // HEAD_PAIR):
        q = jnp.dot(cq_ref[...], w_ref[:, pair * width:(pair + 1) * width], preferred_element_type=F32)
        for h in range(HEAD_PAIR):
            base = h * 2 * nope
            out = pair * width + base
            qn = _rms(q[:, base:base + nope]) * (gn_ref[...] * scale)
            q_ref[:, out:out + nope] = qn.astype(q_ref.dtype)
            qr = _rms(q[:, base + nope:base + 2 * nope]) * (gr_ref[...] * scale)
            q_ref[:, out + nope:out + 2 * nope] = _rope(qr, cos_ref, sin_ref).astype(q_ref.dtype)


def q_proj(u, g_qa, w_q_pad, g_qn, g_qr_pad, cos_t, sin_t, *, n_heads, scale):
    m = u.shape[0]
    lora = g_qa.shape[-1]
    nope = g_qn.shape[-1]
    assert nope == LANES and n_heads % HEAD_PAIR == 0
    tm = _tile(m, 512, SUBLANES)
    whole = lambda shape: pl.BlockSpec(shape, lambda i: (0, 0))
    rows = lambda width: pl.BlockSpec((tm, width), lambda i: (i, 0))
    return pl.pallas_call(
        functools.partial(_q_kernel, heads=n_heads, nope=nope, scale=scale),
        grid=(m // tm,),
        in_specs=[rows(lora), whole((1, lora)), whole(w_q_pad.shape), whole((1, nope)), whole((1, LANES)),
                  rows(LANES), rows(LANES)],
        out_specs=rows(n_heads * 2 * nope),
        out_shape=jax.ShapeDtypeStruct((m, n_heads * 2 * nope), BF16),
        scratch_shapes=[pltpu.VMEM((tm, lora), BF16)],
        compiler_params=_params("parallel"),
        name="q_proj",
    )(u, g_qa.reshape(1, lora), w_q_pad, g_qn.reshape(1, nope), g_qr_pad, cos_t, sin_t)


ATT_TQ = 2048
ATT_TK = 512
ATT_GROUP = 4


def _mla_kernel(*refs, tq, tk, group, has_side):
    if has_side:
        q_ref, k_ref, v_ref, side_in, o_ref, side_out, m_ref, l_ref, acc_ref = refs
        side_out[...] = side_in[...].astype(BF16)
    else:
        q_ref, k_ref, v_ref, o_ref, m_ref, l_ref, acc_ref = refs
    i = pl.program_id(2)
    m_ref[...] = jnp.full(m_ref.shape, -jnp.inf, F32)
    l_ref[...] = jnp.zeros(l_ref.shape, F32)
    acc_ref[...] = jnp.zeros(acc_ref.shape, F32)
    chunks = tk // LANES

    def sub_block(start, diag_offset=None):
        rows = slice(0 if diag_offset is None else diag_offset, tq)
        s = lax.dot_general(q_ref[rows, :], k_ref[pl.ds(start, tk), :], (((1,), (1,)), ((), ())),
                            preferred_element_type=F32)
        if diag_offset is not None:
            row = lax.broadcasted_iota(jnp.int32, (tk, tk), 0)
            col = lax.broadcasted_iota(jnp.int32, (tk, tk), 1)
            top = jnp.where(col <= row, s[:tk], -jnp.inf)
            s = top if s.shape[0] == tk else jnp.concatenate([top, s[tk:]], axis=0)
        m_prev = m_ref[rows, :]
        m_new = jnp.maximum(m_prev, jnp.max(s, axis=-1, keepdims=True))
        alpha = jnp.exp2(m_prev - m_new)
        p = [jnp.exp2(s[:, c * LANES:(c + 1) * LANES] - m_new) for c in range(chunks)]
        l_ref[rows, :] = alpha * l_ref[rows, :] + functools.reduce(lambda a, b: a + b, p)
        pv = jnp.dot(jnp.concatenate(p, axis=1).astype(BF16), v_ref[pl.ds(start, tk), :],
                     preferred_element_type=F32)
        acc_ref[rows, :] = alpha * acc_ref[rows, :] + pv
        m_ref[rows, :] = m_new

    per_tile = tq // tk
    n_off = i * per_tile
    n_groups = n_off // group

    def body(g, carry):
        for t in range(group):
            sub_block(pl.multiple_of((g * group + t) * tk, tk))
        return carry

    lax.fori_loop(0, n_groups, body, 0)
    step = per_tile
    while group % step:
        step -= 1
    for rem in range(step, group, step):
        @pl.when(n_off % group == rem)
        def _(rem=rem):
            for t in range(rem):
                sub_block(pl.multiple_of((n_groups * group + t) * tk, tk))
    for t in range(per_tile):
        sub_block(pl.multiple_of(i * tq + t * tk, tk), diag_offset=t * tk)
    denom = jnp.sum(l_ref[...], axis=-1, keepdims=True)
    o_ref[...] = (acc_ref[...] / denom).astype(o_ref.dtype)


def attention_cast_heads(rows, batch, seq, n_heads):
    nq = seq // _tile(seq, ATT_TQ, ATT_TK)
    for heads in range(n_heads, 0, -1):
        if slab_rows(rows, batch * heads * nq) is not None:
            return heads
    return 0


def mla_attention(q, k, v, *, batch, seq, n_heads, side_cast=None):
    m = q.shape[0]
    dqk = q.shape[1] // n_heads
    dv = v.shape[1] // n_heads
    assert dv == LANES
    tq = _tile(seq, ATT_TQ, ATT_TK)
    tk = min(ATT_TK, tq)
    nq = seq // tq
    in_specs = [pl.BlockSpec((tq, dqk), lambda b, h, i: (b * nq + i, h)),
                pl.BlockSpec((seq, dqk), lambda b, h, i: (b, h)),
                pl.BlockSpec((seq, dv), lambda b, h, i: (b, h))]
    args = [q, k, v]
    out_specs = [pl.BlockSpec((tq, dv), lambda b, h, i: (b * nq + i, h))]
    out_shape = [jax.ShapeDtypeStruct((m, n_heads * dv), BF16)]
    if side_cast is not None:
        src, src_layer = side_cast
        heads = attention_cast_heads(src.shape[1], batch, seq, n_heads)
        slab = slab_rows(src.shape[1], batch * heads * nq)

        def slab_index(b, h, i):
            done = h >= heads
            return (b * heads + jnp.minimum(h, heads - 1)) * nq + jnp.where(done, nq - 1, i)

        in_specs.append(pl.BlockSpec((None, slab, src.shape[2]), lambda b, h, i: (src_layer, slab_index(b, h, i), 0)))
        args.append(src)
        out_specs.append(pl.BlockSpec((slab, src.shape[2]), lambda b, h, i: (slab_index(b, h, i), 0)))
        out_shape.append(jax.ShapeDtypeStruct(src.shape[1:], BF16))
    outs = pl.pallas_call(
        functools.partial(_mla_kernel, tq=tq, tk=tk, group=ATT_GROUP, has_side=side_cast is not None),
        grid=(batch, n_heads, nq),
        in_specs=in_specs,
        out_specs=out_specs,
        out_shape=out_shape,
        scratch_shapes=[pltpu.VMEM((tq, LANES), F32), pltpu.VMEM((tq, LANES), F32), pltpu.VMEM((tq, dv), F32)],
        compiler_params=_params("parallel", "arbitrary", "arbitrary"),
        name="mla_attention",
    )(*args)
    return outs if side_cast is not None else outs[0]


WIDE_TM, WIDE_TN = 2048, 512
OUT_PROJ_TM, OUT_PROJ_TN = 1024, 512
B_IN_TM, B_IN_TN = 1024, 1024
MLP_IN_TM, MLP_IN_TN = 1024, 1024
MLP_OUT_TM, MLP_OUT_TN, MLP_OUT_TK = 1024, 1024, 4096


def _wide(a_parts, w_f32, **kw):
    return matmul(a_parts, w_f32, tm=WIDE_TM, tn=WIDE_TN, single_buffer_a=True, **kw)


def _mlp(x, xg, ssq, w_in_bf16, w_out, layer, emit_stats):
    kw = dict(norm_ssq=ssq, out_dtype=BF16, act="relu2", tm=MLP_IN_TM, tn=MLP_IN_TN, name="mlp_in")
    if side_cast_fits(w_out.shape[1], x.shape[0], w_in_bf16.shape[-1], MLP_IN_TM, MLP_IN_TN):
        hm, w_out_bf16 = matmul([xg], w_in_bf16, side_cast=(w_out, layer), **kw)
    else:
        hm, w_out_bf16 = matmul([xg], w_in_bf16, **kw), w_out[layer].astype(BF16)
    return matmul_ksplit(hm, w_out_bf16, x, tm=MLP_OUT_TM, tn=MLP_OUT_TN, tk=MLP_OUT_TK, emit_stats=emit_stats,
                         name="mlp_out")


def kernel(x, mem, positions, norm_mix, norm_mlp, norm_mem, w_mem_kv, g_mem_q, g_mem_k, w_mlp_in, w_mlp_out,
           a_w_in, a_conv_w, a_conv_b, a_ln_g, a_ln_b, a_w_out, b_w_in, b_g_qa, b_w_uq, b_g_qn, b_g_qr, b_w_out,
           kv_g_in, kv_w_dkv, kv_g_a, kv_w_ukv, kv_g_kn, kv_g_kr):
    batch, seq, d = x.shape
    m = batch * seq
    mem_len = mem.shape[1]
    mem_w = w_mem_kv.shape[-1] // 2
    conv_ch = a_conv_w.shape[-1]
    q_lora = b_g_qa.shape[-1]
    kv_lora = kv_g_a.shape[-1]
    nope, rope = b_g_qn.shape[-1], b_g_qr.shape[-1]
    n_heads = b_w_uq.shape[-1] // (nope + rope)
    dv = kv_w_ukv.shape[-1] // n_heads - nope
    assert 2 * rope == LANES and nope == LANES and dv == LANES
    assert (2 * conv_ch) % mem_w == 0 and q_lora % mem_w == 0
    assert b_w_in.shape[0] == 1 and a_w_in.shape[0] == 1 and norm_mix.shape[0] == 2

    cast = lambda w: w.astype(BF16)
    w_uq = b_w_uq[0].reshape(q_lora, n_heads, nope + rope)
    w_q_pad = cast(jnp.concatenate([w_uq, w_uq[:, :, nope:]], axis=-1).reshape(q_lora, n_heads * 2 * nope))
    w_ukv = kv_w_ukv.reshape(kv_lora, n_heads, nope + dv)
    w_kn = cast(w_ukv[:, :, :nope].reshape(kv_lora, n_heads * nope))
    w_v = cast(w_ukv[:, :, nope:].reshape(kv_lora, n_heads * dv))
    w_dkv_pad = jnp.concatenate([kv_w_dkv, kv_w_dkv[:, kv_lora:]], axis=-1)
    pad_gain = lambda g: jnp.concatenate([g, g]).reshape(1, 2 * rope)

    inv_freq = ROPE_THETA ** (-jnp.arange(0, rope, 2, dtype=F32) / rope)
    ang = positions.astype(F32).reshape(m, 1) * inv_freq
    zeros = jnp.zeros((m, LANES - rope), F32)
    cos_t = jnp.concatenate([jnp.cos(ang), jnp.cos(ang), zeros], axis=-1)
    sin_t = jnp.concatenate([-jnp.sin(ang), jnp.sin(ang), zeros], axis=-1)

    x = x.reshape(m, d)
    kv_mem = mem_kv(mem.reshape(batch * mem_len, d), norm_mem, w_mem_kv, g_mem_k)

    out_proj = functools.partial(matmul, out_dtype=F32, tm=OUT_PROJ_TM, tn=OUT_PROJ_TN, emit_stats=True)
    h, = rmsnorm_cast(x, norm_mix[0:1])
    u = _wide([h], a_w_in[0], out_dtype=F32, name="a_in_proj")
    conv_args = (u, a_conv_w[0], a_conv_b[0], a_ln_g[0], a_ln_b[0])
    if slab_rows(w_mlp_in.shape[1], m // _tile(seq, CONV_TS, CONV_HALO)) is not None:
        y_main, w_in0 = conformer_conv(*conv_args, seq=seq, side_cast=(w_mlp_in, 0))
    else:
        y_main, w_in0 = conformer_conv(*conv_args, seq=seq), cast(w_mlp_in[0])
    y_mem = mem_attention(u, 2 * conv_ch // mem_w, kv_mem, 0, g_mem_q[0], seq=seq, mem_len=mem_len)
    x, xg, ssq = out_proj([y_main, y_mem], cast(a_w_out[0]), residual=x, next_gain=norm_mlp[0], name="a_out_proj")
    x, xb, ssq = _mlp(x, xg, ssq, w_in0, w_mlp_out, 0, emit_stats=True)

    c_kv, k_rope = kv_down(xb, ssq, scale_rows_cast(w_dkv_pad, kv_g_in), kv_g_a, pad_gain(kv_g_kr), cos_t, sin_t)
    k_all, v_all = kv_up(c_kv, w_kn, w_v, k_rope, kv_g_kn, n_heads=n_heads)
    u = matmul([xb], scale_rows_cast(b_w_in[0], norm_mix[1]), norm_ssq=ssq, out_dtype=F32, tm=B_IN_TM, tn=B_IN_TN,
               name="b_in_proj")
    q_all = q_proj(u, b_g_qa[0], w_q_pad, b_g_qn[0], pad_gain(b_g_qr[0]), cos_t, sin_t,
                   n_heads=n_heads, scale=(nope + rope) ** -0.5 * math.log2(math.e))
    if attention_cast_heads(w_mlp_in.shape[1], batch, seq, n_heads):
        y_main, w_in1 = mla_attention(q_all, k_all, v_all, batch=batch, seq=seq, n_heads=n_heads,
                                      side_cast=(w_mlp_in, 1))
    else:
        y_main = mla_attention(q_all, k_all, v_all, batch=batch, seq=seq, n_heads=n_heads)
        w_in1 = cast(w_mlp_in[1])
    y_mem = mem_attention(u, q_lora // mem_w, kv_mem, 1, g_mem_q[1], seq=seq, mem_len=mem_len)
    x, xg, ssq = out_proj([y_main, y_mem], cast(b_w_out[0]), residual=x, next_gain=norm_mlp[1], name="b_out_proj")
    x = _mlp(x, xg, ssq, w_in1, w_mlp_out, 1, emit_stats=False)
    return x.reshape(batch, seq, d)
```

```python
import functools
import math

import jax
import jax.numpy as jnp
from jax import lax
from jax.experimental import pallas as pl
from jax.experimental.pallas import tpu as pltpu

EPS = 1e-6
ROPE_THETA = 10000.0
LANES = 128
SUBLANES = 8
BF16_SUBLANES = 16
MIB = 1 << 20
VMEM_LIMIT_BYTES = 56 * MIB
KSPLIT_STATS_VMEM_BYTES = 62 * MIB
F32 = jnp.float32
BF16 = jnp.bfloat16


def _tile(n, pref, mult=LANES):
    if n <= pref:
        return n
    t = (pref // mult) * mult
    while t >= mult:
        if n % t == 0:
            return t
        t -= mult
    raise ValueError(f"no tile for {n} (pref {pref}, mult {mult})")


def _params(*semantics, vmem_limit_bytes=VMEM_LIMIT_BYTES):
    return pltpu.CompilerParams(dimension_semantics=semantics, vmem_limit_bytes=vmem_limit_bytes)


def _rms(x):
    ms = jnp.sum(x * x, axis=-1, keepdims=True) * (1.0 / x.shape[-1])
    return x * lax.rsqrt(ms + EPS)


def _rmsnorm_kernel(x_ref, g_ref, *o_refs):
    xn = _rms(x_ref[...])
    for i, o_ref in enumerate(o_refs):
        o_ref[...] = (xn * g_ref[i:i + 1, :]).astype(o_ref.dtype)


def rmsnorm_cast(x, gains):
    m, d = x.shape
    n = gains.shape[0]
    tm = _tile(m, 512, SUBLANES)
    return pl.pallas_call(
        _rmsnorm_kernel,
        grid=(m // tm,),
        in_specs=[pl.BlockSpec((tm, d), lambda i: (i, 0)),
                  pl.BlockSpec((n, d), lambda i: (0, 0))],
        out_specs=[pl.BlockSpec((tm, d), lambda i: (i, 0)) for _ in range(n)],
        out_shape=[jax.ShapeDtypeStruct((m, d), BF16) for _ in range(n)],
        compiler_params=_params("parallel"),
        name="rmsnorm_cast",
    )(x, gains)


def _scale_cast_kernel(w_ref, g_ref, o_ref):
    g = g_ref[...]
    for c in range(w_ref.shape[1] // LANES):
        cols = slice(c * LANES, (c + 1) * LANES)
        o_ref[:, cols] = (w_ref[:, cols] * g).astype(o_ref.dtype)


def scale_rows_cast(w, gain):
    k, n = w.shape
    tk = _tile(k, 512, SUBLANES)
    return pl.pallas_call(
        _scale_cast_kernel,
        grid=(k // tk,),
        in_specs=[pl.BlockSpec((tk, n), lambda i: (i, 0)), pl.BlockSpec((tk, LANES), lambda i: (i, 0))],
        out_specs=pl.BlockSpec((tk, n), lambda i: (i, 0)),
        out_shape=jax.ShapeDtypeStruct((k, n), BF16),
        compiler_params=_params("parallel"),
        name="scale_rows_cast",
    )(w, jnp.broadcast_to(gain.reshape(k, 1), (k, LANES)))


def _row_stats(x, xb_ref, ssq_ref, first_col_tile, gain=None):
    xb_ref[...] = (x if gain is None else x * gain).astype(xb_ref.dtype)
    sq = x * x
    partial = functools.reduce(lambda p, q: p + q,
                               [sq[:, c * LANES:(c + 1) * LANES] for c in range(x.shape[1] // LANES)])

    @pl.when(first_col_tile)
    def _():
        ssq_ref[...] = partial

    @pl.when(jnp.logical_not(first_col_tile))
    def _():
        ssq_ref[...] += partial


def _matmul_kernel(*refs, names, act, norm_dim):
    r = dict(zip(names, refs))
    if "side_in" in r:
        r["side_out"][...] = r["side_in"][...].astype(BF16)
    acc = None
    for key in names:
        if not key.startswith("a"):
            continue
        part = jnp.dot(r[key][...], r["w" + key[1:]][...].astype(BF16), preferred_element_type=F32)
        acc = part if acc is None else acc + part
    if "ssq" in r:
        acc = acc * lax.rsqrt(jnp.sum(r["ssq"][...], axis=-1, keepdims=True) * (1.0 / norm_dim) + EPS)
    if act == "relu2":
        acc = jnp.square(jnp.maximum(acc, 0.0))
    if "res" in r:
        acc = acc + r["res"][...]
    r["out"][...] = acc.astype(r["out"].dtype)
    if "xb" in r:
        _row_stats(acc, r["xb"], r["ssq_out"], pl.program_id(1) == 0,
                   gain=r["next_gain"][...] if "next_gain" in r else None)


def matmul(a_parts, w, *, out_dtype, act=None, residual=None, tm, tn, single_buffer_a=False,
           side_cast=None, norm_ssq=None, emit_stats=False, next_gain=None, name):
    m = a_parts[0].shape[0]
    n = w.shape[-1]
    tm, tn = _tile(m, tm, SUBLANES), _tile(n, tn)
    nj = n // tn
    a_mode = dict(pipeline_mode=pl.Buffered(1)) if single_buffer_a else {}
    names, in_specs, args = [], [], []

    def add(name_, spec, arr):
        names.append(name_)
        in_specs.append(spec)
        args.append(arr)

    offset = 0
    for p, a in enumerate(a_parts):
        kp = a.shape[1]
        assert offset % kp == 0
        add(f"a{p}", pl.BlockSpec((tm, kp), lambda i, j: (i, 0), **a_mode), a)
        add(f"w{p}", pl.BlockSpec((kp, tn), lambda i, j, blk=offset // kp: (blk, j)), w)
        offset += kp
    kdim = offset
    assert kdim == w.shape[-2]
    if norm_ssq is not None:
        assert len(a_parts) == 1
        add("ssq", pl.BlockSpec((tm, LANES), lambda i, j: (i, 0)), norm_ssq)
    if next_gain is not None:
        assert emit_stats
        add("next_gain", pl.BlockSpec((1, tn), lambda i, j: (0, j)), next_gain.reshape(1, n))
    if residual is not None:
        add("res", pl.BlockSpec((tm, tn), lambda i, j: (i, j)), residual)
    if side_cast is not None:
        src, src_layer = side_cast
        slab = src.shape[1] // ((m // tm) * nj)
        add("side_in", pl.BlockSpec((None, slab, src.shape[2]), lambda i, j: (src_layer, i * nj + j, 0)), src)
    names.append("out")
    out_specs = [pl.BlockSpec((tm, tn), lambda i, j: (i, j))]
    out_shape = [jax.ShapeDtypeStruct((m, n), out_dtype)]
    if side_cast is not None:
        names.append("side_out")
        out_specs.append(pl.BlockSpec((slab, src.shape[2]), lambda i, j: (i * nj + j, 0)))
        out_shape.append(jax.ShapeDtypeStruct(src.shape[1:], BF16))
    if emit_stats:
        assert out_dtype == F32
        names += ["xb", "ssq_out"]
        out_specs += [pl.BlockSpec((tm, tn), lambda i, j: (i, j)), pl.BlockSpec((tm, LANES), lambda i, j: (i, 0))]
        out_shape += [jax.ShapeDtypeStruct((m, n), BF16), jax.ShapeDtypeStruct((m, LANES), F32)]
    outs = pl.pallas_call(
        functools.partial(_matmul_kernel, names=tuple(names), act=act, norm_dim=kdim),
        grid=(m // tm, nj),
        in_specs=in_specs,
        out_specs=out_specs,
        out_shape=out_shape,
        compiler_params=_params("parallel", "arbitrary"),
        name=name,
    )(*args)
    return outs if len(outs) > 1 else outs[0]


def side_cast_fits(rows, m, n, tm, tn):
    return slab_rows(rows, (m // _tile(m, tm, SUBLANES)) * (n // _tile(n, tn))) is not None


def slab_rows(rows, steps):
    if steps <= 0 or rows % steps or (rows // steps) % BF16_SUBLANES:
        return None
    return rows // steps


def _matmul_ksplit_kernel(a_ref, w_ref, r_ref, o_ref, *stats_refs):
    k = pl.program_id(2)

    @pl.when(k == 0)
    def _():
        o_ref[...] = r_ref[...]

    o_ref[...] += jnp.dot(a_ref[...], w_ref[...], preferred_element_type=F32)
    if stats_refs:
        @pl.when(k == pl.num_programs(2) - 1)
        def _():
            _row_stats(o_ref[...], *stats_refs, pl.program_id(1) == 0)


def matmul_ksplit(a, w, residual, *, tm, tn, tk, emit_stats=False, name):
    m, kdim = a.shape
    n = w.shape[1]
    tm, tn, tk = _tile(m, tm, SUBLANES), _tile(n, tn), _tile(kdim, tk)
    nj = n // tn
    out_specs = [pl.BlockSpec((tm, tn), lambda i, j, k: (i, j))]
    out_shape = [jax.ShapeDtypeStruct((m, n), F32)]
    if emit_stats:
        out_specs += [pl.BlockSpec((tm, tn), lambda i, j, k: (i, j)),
                      pl.BlockSpec((tm, LANES), lambda i, j, k: (i, 0))]
        out_shape += [jax.ShapeDtypeStruct((m, n), BF16), jax.ShapeDtypeStruct((m, LANES), F32)]
    outs = pl.pallas_call(
        _matmul_ksplit_kernel,
        grid=(m // tm, nj, kdim // tk),
        in_specs=[pl.BlockSpec((tm, tk), lambda i, j, k: (i, k)),
                  pl.BlockSpec((tk, tn), lambda i, j, k: (k, j)),
                  pl.BlockSpec((tm, tn), lambda i, j, k: (i, j))],
        out_specs=out_specs,
        out_shape=out_shape,
        compiler_params=_params("parallel", "arbitrary", "arbitrary",
                                vmem_limit_bytes=KSPLIT_STATS_VMEM_BYTES if emit_stats else VMEM_LIMIT_BYTES),
        name=name,
    )(a, w, residual)
    return outs if emit_stats else outs[0]


CONV_ROWS = 64
CONV_HALO = 32
CONV_TS = 256
CONV_LANES = 256


def _sigmoid(x):
    return 0.5 * jnp.tanh(0.5 * x) + 0.5


def _conv_kernel(*refs, ts, seq, cw, lc, n_side):
    a_ref, gate_ref, w_ref, cb_ref, lng_ref, lnb_ref = refs[:6]
    side_ins = refs[6:6 + n_side]
    o_ref = refs[6 + n_side]
    side_outs = refs[7 + n_side:7 + 2 * n_side]
    buf_ref, c_ref = refs[7 + 2 * n_side:]
    for side_in, side_out in zip(side_ins, side_outs):
        side_out[...] = side_in[...].astype(BF16)
    i = pl.program_id(0)
    at_seq_start = (i * ts) % seq == 0

    @pl.when(at_seq_start)
    def _():
        buf_ref[0:CONV_HALO, :] = jnp.zeros((CONV_HALO, buf_ref.shape[1]), F32)

    @pl.when(jnp.logical_not(at_seq_start))
    def _():
        buf_ref[0:CONV_HALO, :] = buf_ref[ts:ts + CONV_HALO, :]

    buf_ref[CONV_HALO:CONV_HALO + ts, :] = a_ref[...] * _sigmoid(gate_ref[...])
    lead = CONV_HALO - (cw - 1)
    n_ch = a_ref.shape[1]

    def chunk(r, carry):
        r0 = pl.multiple_of(r * CONV_ROWS, CONV_ROWS)
        for c0 in range(0, n_ch, lc):
            cols = slice(c0, c0 + lc)
            acc = None
            for phase in range(SUBLANES):
                taps = [j for j in range(cw) if (lead + j) % SUBLANES == phase]
                if not taps:
                    continue
                rows = CONV_ROWS + (SUBLANES if phase else 0)
                part = None
                for j in taps:
                    base = (lead + j) // SUBLANES * SUBLANES
                    x = buf_ref[pl.ds(r0 + base, rows), cols].reshape(rows // SUBLANES, SUBLANES, lc)
                    term = x * w_ref[j, :, cols][None]
                    part = term if part is None else part + term
                part = part.reshape(rows, lc)
                piece = part[phase:phase + CONV_ROWS] if phase else part
                acc = piece if acc is None else acc + piece
            c_ref[pl.ds(r0, CONV_ROWS), cols] = acc + cb_ref[:, cols]
        c = c_ref[pl.ds(r0, CONV_ROWS), :]
        mu = jnp.mean(c, axis=-1, keepdims=True)
        cc = c - mu
        var = jnp.mean(cc * cc, axis=-1, keepdims=True)
        y = cc * lax.rsqrt(var + EPS) * lng_ref[...] + lnb_ref[...]
        o_ref[pl.ds(r0, CONV_ROWS), :] = (y * _sigmoid(y)).astype(o_ref.dtype)
        return carry

    lax.fori_loop(0, ts // CONV_ROWS, chunk, 0)


def conformer_conv(u, conv_w, conv_b, ln_g, ln_b, *, seq, side_casts=()):
    m = u.shape[0]
    cw, c = conv_w.shape
    assert cw - 1 <= CONV_HALO
    ts = _tile(seq, CONV_TS, CONV_HALO)
    lc = _tile(c, CONV_LANES)
    row = lambda v: v.reshape(1, c)
    in_specs = [pl.BlockSpec((ts, c), lambda i: (i, 0)),
                pl.BlockSpec((ts, c), lambda i: (i, 1)),
                pl.BlockSpec((cw, SUBLANES, c), lambda i: (0, 0, 0)),
                pl.BlockSpec((1, c), lambda i: (0, 0)),
                pl.BlockSpec((1, c), lambda i: (0, 0)),
                pl.BlockSpec((1, c), lambda i: (0, 0))]
    args = [u, u, jnp.broadcast_to(conv_w[:, None, :], (cw, SUBLANES, c)), row(conv_b), row(ln_g), row(ln_b)]
    out_specs = [pl.BlockSpec((ts, c), lambda i: (i, 0))]
    out_shape = [jax.ShapeDtypeStruct((m, c), BF16)]
    for src, src_layer in side_casts:
        slab = slab_rows(src.shape[1], m // ts)
        in_specs.append(pl.BlockSpec((None, slab, src.shape[2]), lambda i, l=src_layer: (l, i, 0)))
        args.append(src)
        out_specs.append(pl.BlockSpec((slab, src.shape[2]), lambda i: (i, 0)))
        out_shape.append(jax.ShapeDtypeStruct(src.shape[1:], BF16))
    outs = pl.pallas_call(
        functools.partial(_conv_kernel, ts=ts, seq=seq, cw=cw, lc=lc, n_side=len(side_casts)),
        grid=(m // ts,),
        in_specs=in_specs,
        out_specs=out_specs,
        out_shape=out_shape,
        scratch_shapes=[pltpu.VMEM((CONV_HALO + ts, c), F32), pltpu.VMEM((ts, c), F32)],
        compiler_params=_params("arbitrary"),
        name="conformer_conv",
    )(*args)
    return outs if side_casts else outs[0]


def _mem_kv_kernel(mem_ref, gn_ref, w_ref, gk_ref, o_ref, *, n_k_heads):
    j = pl.program_id(1)
    hn = (_rms(mem_ref[...]) * gn_ref[0]).astype(BF16)
    kv = jnp.dot(hn, w_ref[0].astype(BF16), preferred_element_type=F32)
    kn = _rms(kv) * gk_ref[0]
    o_ref[0] = jnp.where(j < n_k_heads, kn, kv).astype(o_ref.dtype)


def mem_kv(mem2d, norm_mem, w_mem_kv, g_mem_k):
    nl, d, n2 = w_mem_kv.shape
    hd = g_mem_k.shape[-1]
    bm = mem2d.shape[0]
    return pl.pallas_call(
        functools.partial(_mem_kv_kernel, n_k_heads=n2 // 2 // hd),
        grid=(nl, n2 // hd),
        in_specs=[pl.BlockSpec((bm, d), lambda l, j: (0, 0)),
                  pl.BlockSpec((1, 1, d), lambda l, j: (l, 0, 0)),
                  pl.BlockSpec((1, d, hd), lambda l, j: (l, 0, j)),
                  pl.BlockSpec((1, 1, hd), lambda l, j: (l, 0, 0))],
        out_specs=pl.BlockSpec((1, bm, hd), lambda l, j: (l, 0, j)),
        out_shape=jax.ShapeDtypeStruct((nl, bm, n2), BF16),
        compiler_params=_params("parallel", "parallel"),
        name="mem_kv",
    )(mem2d, norm_mem.reshape(nl, 1, d), w_mem_kv, g_mem_k.reshape(nl, 1, hd))


def _mem_attn_kernel(q_ref, k_ref, v_ref, gq_ref, o_ref, *, n_heads, hd):
    scale = hd ** -0.5
    for h in range(n_heads):
        cols = slice(h * hd, (h + 1) * hd)
        q = (_rms(q_ref[:, cols]) * (gq_ref[...] * scale)).astype(BF16)
        s = lax.dot_general(q, k_ref[0, :, cols], (((1,), (1,)), ((), ())), preferred_element_type=F32)
        p = jnp.exp(s - jnp.max(s, axis=-1, keepdims=True))
        denom = jnp.sum(p, axis=-1, keepdims=True)
        o = jnp.dot(p.astype(BF16), v_ref[0, :, cols], preferred_element_type=F32)
        o_ref[:, cols] = (o / denom).astype(o_ref.dtype)


def mem_attention(qsrc, q_col_block, kv, layer, g_q, *, seq, mem_len):
    m = qsrc.shape[0]
    w = kv.shape[-1] // 2
    hd = g_q.shape[-1]
    ts = _tile(seq, 512, SUBLANES)
    tiles_per_seq = seq // ts
    return pl.pallas_call(
        functools.partial(_mem_attn_kernel, n_heads=w // hd, hd=hd),
        grid=(m // ts,),
        in_specs=[pl.BlockSpec((ts, w), lambda i: (i, q_col_block)),
                  pl.BlockSpec((1, mem_len, w), lambda i: (layer, i // tiles_per_seq, 0)),
                  pl.BlockSpec((1, mem_len, w), lambda i: (layer, i // tiles_per_seq, 1)),
                  pl.BlockSpec((1, hd), lambda i: (0, 0))],
        out_specs=pl.BlockSpec((ts, w), lambda i: (i, 0)),
        out_shape=jax.ShapeDtypeStruct((m, w), BF16),
        compiler_params=_params("parallel"),
        name="mem_attention",
    )(qsrc, kv, kv, g_q.reshape(1, hd))


def _rope(t, cos_ref, sin_ref):
    partner = pltpu.roll(t, LANES // 4, 1)
    return t * cos_ref[...] + partner * sin_ref[...]


def _dkv_kernel(xb_ref, ssq_ref, w_ref, ga_ref, gr_ref, cos_ref, sin_ref, ckv_ref, kr_ref, *, lora, dim):
    ckr = jnp.dot(xb_ref[...], w_ref[...], preferred_element_type=F32)
    ckr = ckr * lax.rsqrt(jnp.sum(ssq_ref[...], axis=-1, keepdims=True) * (1.0 / dim) + EPS)
    ckv_ref[...] = (_rms(ckr[:, :lora]) * ga_ref[...]).astype(ckv_ref.dtype)
    kr = _rms(ckr[:, lora:]) * gr_ref[...]
    kr_ref[...] = _rope(kr, cos_ref, sin_ref).astype(kr_ref.dtype)


def kv_down(xb, ssq, w_dkv_pad, g_a, g_kr_pad, cos_t, sin_t):
    m, d = xb.shape
    lora = g_a.shape[-1]
    tm = _tile(m, 1024, SUBLANES)
    full = lambda shape: pl.BlockSpec(shape, lambda i: (0, 0))
    rows = lambda width: pl.BlockSpec((tm, width), lambda i: (i, 0))
    return pl.pallas_call(
        functools.partial(_dkv_kernel, lora=lora, dim=d),
        grid=(m // tm,),
        in_specs=[rows(d), rows(LANES), full(w_dkv_pad.shape), full((1, lora)), full((1, LANES)),
                  rows(LANES), rows(LANES)],
        out_specs=[rows(lora), rows(LANES)],
        out_shape=[jax.ShapeDtypeStruct((m, lora), BF16), jax.ShapeDtypeStruct((m, LANES), BF16)],
        compiler_params=_params("parallel"),
        name="kv_down",
    )(xb, ssq, w_dkv_pad, g_a.reshape(1, lora), g_kr_pad, cos_t, sin_t)


HEAD_PAIR = 2


def _ukv_kernel(c_ref, wk_ref, wv_ref, kr_ref, gk_ref, k_ref, v_ref, *, heads, nope):
    c = c_ref[...]
    kr = kr_ref[...]
    width = HEAD_PAIR * nope
    for pair in range(heads // HEAD_PAIR):
        kn = jnp.dot(c, wk_ref[:, pair * width:(pair + 1) * width], preferred_element_type=F32)
        for h in range(HEAD_PAIR):
            base = (pair * HEAD_PAIR + h) * 2 * nope
            k_ref[:, base:base + nope] = (_rms(kn[:, h * nope:(h + 1) * nope]) * gk_ref[...]).astype(k_ref.dtype)
            k_ref[:, base + nope:base + 2 * nope] = kr
        cols = slice(pair * width, (pair + 1) * width)
        v_ref[:, cols] = jnp.dot(c, wv_ref[:, cols], preferred_element_type=F32).astype(v_ref.dtype)


def kv_up(c_kv, w_kn, w_v, kr, g_kn, *, n_heads):
    m, lora = c_kv.shape
    nope = g_kn.shape[-1]
    dv = w_v.shape[1] // n_heads
    assert nope == LANES and dv == LANES and n_heads % HEAD_PAIR == 0
    tm = _tile(m, 512, SUBLANES)
    whole = lambda shape: pl.BlockSpec(shape, lambda i: (0, 0))
    rows = lambda width: pl.BlockSpec((tm, width), lambda i: (i, 0))
    return pl.pallas_call(
        functools.partial(_ukv_kernel, heads=n_heads, nope=nope),
        grid=(m // tm,),
        in_specs=[rows(lora), whole(w_kn.shape), whole(w_v.shape), rows(LANES), whole((1, nope))],
        out_specs=[rows(n_heads * 2 * nope), rows(n_heads * dv)],
        out_shape=[jax.ShapeDtypeStruct((m, n_heads * 2 * nope), BF16),
                   jax.ShapeDtypeStruct((m, n_heads * dv), BF16)],
        compiler_params=_params("parallel"),
        name="kv_up",
    )(c_kv, w_kn, w_v, kr, g_kn.reshape(1, nope))


def _q_kernel(*refs, heads, nope, scale, has_side):
    if has_side:
        u_ref, ga_ref, w_ref, gn_ref, gr_ref, cos_ref, sin_ref, side_in, q_ref, side_out, cq_ref = refs
        side_out[...] = side_in[...].astype(BF16)
    else:
        u_ref, ga_ref, w_ref, gn_ref, gr_ref, cos_ref, sin_ref, q_ref, cq_ref = refs
    cq_ref[...] = (_rms(u_ref[...]) * ga_ref[...]).astype(cq_ref.dtype)
    width = HEAD_PAIR * 2 * nope
    for pair in range(heads // HEAD_PAIR):
        q = jnp.dot(cq_ref[...], w_ref[:, pair * width:(pair + 1) * width], preferred_element_type=F32)
        for h in range(HEAD_PAIR):
            base = h * 2 * nope
            out = pair * width + base
            qn = _rms(q[:, base:base + nope]) * (gn_ref[...] * scale)
            q_ref[:, out:out + nope] = qn.astype(q_ref.dtype)
            qr = _rms(q[:, base + nope:base + 2 * nope]) * (gr_ref[...] * scale)
            q_ref[:, out + nope:out + 2 * nope] = _rope(qr, cos_ref, sin_ref).astype(q_ref.dtype)


def q_proj(u, g_qa, w_q_pad, g_qn, g_qr_pad, cos_t, sin_t, *, n_heads, scale, side_cast=None):
    m = u.shape[0]
    lora = g_qa.shape[-1]
    nope = g_qn.shape[-1]
    assert nope == LANES and n_heads % HEAD_PAIR == 0
    tm = _tile(m, 512, SUBLANES)
    whole = lambda shape: pl.BlockSpec(shape, lambda i: (0, 0))
    rows_of = lambda height, width: pl.BlockSpec((height, width), lambda i: (i, 0))
    rows = lambda width: rows_of(tm, width)
    in_specs = [rows(lora), whole((1, lora)), whole(w_q_pad.shape), whole((1, nope)), whole((1, LANES)),
                rows(LANES), rows(LANES)]
    args = [u, g_qa.reshape(1, lora), w_q_pad, g_qn.reshape(1, nope), g_qr_pad, cos_t, sin_t]
    out_specs = [rows(n_heads * 2 * nope)]
    out_shape = [jax.ShapeDtypeStruct((m, n_heads * 2 * nope), BF16)]
    if side_cast is not None:
        slab = slab_rows(side_cast.shape[0], m // tm)
        in_specs.append(rows_of(slab, side_cast.shape[1]))
        args.append(side_cast)
        out_specs.append(rows_of(slab, side_cast.shape[1]))
        out_shape.append(jax.ShapeDtypeStruct(side_cast.shape, BF16))
    outs = pl.pallas_call(
        functools.partial(_q_kernel, heads=n_heads, nope=nope, scale=scale, has_side=side_cast is not None),
        grid=(m // tm,),
        in_specs=in_specs,
        out_specs=out_specs,
        out_shape=out_shape,
        scratch_shapes=[pltpu.VMEM((tm, lora), BF16)],
        compiler_params=_params("parallel"),
        name="q_proj",
    )(*args)
    return outs if side_cast is not None else outs[0]


ATT_TQ = 2048
ATT_TK = 512
ATT_GROUP = 4


def _mla_kernel(*refs, tq, tk, group, has_side):
    if has_side:
        q_ref, k_ref, v_ref, side_in, o_ref, side_out, m_ref, l_ref, acc_ref = refs
        side_out[...] = side_in[...].astype(BF16)
    else:
        q_ref, k_ref, v_ref, o_ref, m_ref, l_ref, acc_ref = refs
    i = pl.program_id(2)
    m_ref[...] = jnp.full(m_ref.shape, -jnp.inf, F32)
    l_ref[...] = jnp.zeros(l_ref.shape, F32)
    acc_ref[...] = jnp.zeros(acc_ref.shape, F32)
    chunks = tk // LANES

    def sub_block(start, diag_offset=None):
        rows = slice(0 if diag_offset is None else diag_offset, tq)
        s = lax.dot_general(q_ref[rows, :], k_ref[pl.ds(start, tk), :], (((1,), (1,)), ((), ())),
                            preferred_element_type=F32)
        if diag_offset is not None:
            row = lax.broadcasted_iota(jnp.int32, (tk, tk), 0)
            col = lax.broadcasted_iota(jnp.int32, (tk, tk), 1)
            top = jnp.where(col <= row, s[:tk], -jnp.inf)
            s = top if s.shape[0] == tk else jnp.concatenate([top, s[tk:]], axis=0)
        m_prev = m_ref[rows, :]
        m_new = jnp.maximum(m_prev, jnp.max(s, axis=-1, keepdims=True))
        alpha = jnp.exp2(m_prev - m_new)
        p = [jnp.exp2(s[:, c * LANES:(c + 1) * LANES] - m_new) for c in range(chunks)]
        l_ref[rows, :] = alpha * l_ref[rows, :] + functools.reduce(lambda a, b: a + b, p)
        pv = jnp.dot(jnp.concatenate(p, axis=1).astype(BF16), v_ref[pl.ds(start, tk), :],
                     preferred_element_type=F32)
        acc_ref[rows, :] = alpha * acc_ref[rows, :] + pv
        m_ref[rows, :] = m_new

    per_tile = tq // tk
    n_off = i * per_tile
    n_groups = n_off // group

    def body(g, carry):
        for t in range(group):
            sub_block(pl.multiple_of((g * group + t) * tk, tk))
        return carry

    lax.fori_loop(0, n_groups, body, 0)
    step = per_tile
    while group % step:
        step -= 1
    for rem in range(step, group, step):
        @pl.when(n_off % group == rem)
        def _(rem=rem):
            for t in range(rem):
                sub_block(pl.multiple_of((n_groups * group + t) * tk, tk))
    for t in range(per_tile):
        sub_block(pl.multiple_of(i * tq + t * tk, tk), diag_offset=t * tk)
    denom = jnp.sum(l_ref[...], axis=-1, keepdims=True)
    o_ref[...] = (acc_ref[...] / denom).astype(o_ref.dtype)


def attention_cast_heads(rows, batch, seq, n_heads):
    nq = seq // _tile(seq, ATT_TQ, ATT_TK)
    for heads in range(n_heads, 0, -1):
        if slab_rows(rows, batch * heads * nq) is not None:
            return heads
    return 0


def mla_attention(q, k, v, *, batch, seq, n_heads, side_cast=None):
    m = q.shape[0]
    dqk = q.shape[1] // n_heads
    dv = v.shape[1] // n_heads
    assert dv == LANES
    tq = _tile(seq, ATT_TQ, ATT_TK)
    tk = min(ATT_TK, tq)
    nq = seq // tq
    in_specs = [pl.BlockSpec((tq, dqk), lambda b, h, i: (b * nq + i, h)),
                pl.BlockSpec((seq, dqk), lambda b, h, i: (b, h)),
                pl.BlockSpec((seq, dv), lambda b, h, i: (b, h))]
    args = [q, k, v]
    out_specs = [pl.BlockSpec((tq, dv), lambda b, h, i: (b * nq + i, h))]
    out_shape = [jax.ShapeDtypeStruct((m, n_heads * dv), BF16)]
    if side_cast is not None:
        src, src_layer = side_cast
        heads = attention_cast_heads(src.shape[1], batch, seq, n_heads)
        slab = slab_rows(src.shape[1], batch * heads * nq)

        def slab_index(b, h, i):
            done = h >= heads
            return (b * heads + jnp.minimum(h, heads - 1)) * nq + jnp.where(done, nq - 1, i)

        in_specs.append(pl.BlockSpec((None, slab, src.shape[2]), lambda b, h, i: (src_layer, slab_index(b, h, i), 0)))
        args.append(src)
        out_specs.append(pl.BlockSpec((slab, src.shape[2]), lambda b, h, i: (slab_index(b, h, i), 0)))
        out_shape.append(jax.ShapeDtypeStruct(src.shape[1:], BF16))
    outs = pl.pallas_call(
        functools.partial(_mla_kernel, tq=tq, tk=tk, group=ATT_GROUP, has_side=side_cast is not None),
        grid=(batch, n_heads, nq),
        in_specs=in_specs,
        out_specs=out_specs,
        out_shape=out_shape,
        scratch_shapes=[pltpu.VMEM((tq, LANES), F32), pltpu.VMEM((tq, LANES), F32), pltpu.VMEM((tq, dv), F32)],
        compiler_params=_params("parallel", "arbitrary", "arbitrary"),
        name="mla_attention",
    )(*args)
    return outs if side_cast is not None else outs[0]


WIDE_TM, WIDE_TN = 2048, 512
OUT_PROJ_TM, OUT_PROJ_TN = 1024, 512
B_IN_TM, B_IN_TN = 1024, 1024
MLP_IN_TM, MLP_IN_TN = 1024, 1024
MLP_OUT_TM, MLP_OUT_TN, MLP_OUT_TK = 1024, 1024, 4096


def _wide(a_parts, w_f32, **kw):
    return matmul(a_parts, w_f32, tm=WIDE_TM, tn=WIDE_TN, single_buffer_a=True, **kw)


def _mlp(x, xg, ssq, w_in_bf16, w_out, layer, emit_stats):
    kw = dict(norm_ssq=ssq, out_dtype=BF16, act="relu2", tm=MLP_IN_TM, tn=MLP_IN_TN, name="mlp_in")
    if side_cast_fits(w_out.shape[1], x.shape[0], w_in_bf16.shape[-1], MLP_IN_TM, MLP_IN_TN):
        hm, w_out_bf16 = matmul([xg], w_in_bf16, side_cast=(w_out, layer), **kw)
    else:
        hm, w_out_bf16 = matmul([xg], w_in_bf16, **kw), w_out[layer].astype(BF16)
    return matmul_ksplit(hm, w_out_bf16, x, tm=MLP_OUT_TM, tn=MLP_OUT_TN, tk=MLP_OUT_TK, emit_stats=emit_stats,
                         name="mlp_out")


def kernel(x, mem, positions, norm_mix, norm_mlp, norm_mem, w_mem_kv, g_mem_q, g_mem_k, w_mlp_in, w_mlp_out,
           a_w_in, a_conv_w, a_conv_b, a_ln_g, a_ln_b, a_w_out, b_w_in, b_g_qa, b_w_uq, b_g_qn, b_g_qr, b_w_out,
           kv_g_in, kv_w_dkv, kv_g_a, kv_w_ukv, kv_g_kn, kv_g_kr):
    batch, seq, d = x.shape
    m = batch * seq
    mem_len = mem.shape[1]
    mem_w = w_mem_kv.shape[-1] // 2
    conv_ch = a_conv_w.shape[-1]
    q_lora = b_g_qa.shape[-1]
    kv_lora = kv_g_a.shape[-1]
    nope, rope = b_g_qn.shape[-1], b_g_qr.shape[-1]
    n_heads = b_w_uq.shape[-1] // (nope + rope)
    dv = kv_w_ukv.shape[-1] // n_heads - nope
    assert 2 * rope == LANES and nope == LANES and dv == LANES
    assert (2 * conv_ch) % mem_w == 0 and q_lora % mem_w == 0
    assert b_w_in.shape[0] == 1 and a_w_in.shape[0] == 1 and norm_mix.shape[0] == 2

    cast = lambda w: w.astype(BF16)
    w_uq = b_w_uq[0].reshape(q_lora, n_heads, nope + rope)
    w_q_pad = cast(jnp.concatenate([w_uq, w_uq[:, :, nope:]], axis=-1).reshape(q_lora, n_heads * 2 * nope))
    w_ukv = kv_w_ukv.reshape(kv_lora, n_heads, nope + dv)
    w_kn = cast(w_ukv[:, :, :nope].reshape(kv_lora, n_heads * nope))
    w_v = cast(w_ukv[:, :, nope:].reshape(kv_lora, n_heads * dv))
    w_dkv_pad = jnp.concatenate([kv_w_dkv, kv_w_dkv[:, kv_lora:]], axis=-1)
    pad_gain = lambda g: jnp.concatenate([g, g]).reshape(1, 2 * rope)

    inv_freq = ROPE_THETA ** (-jnp.arange(0, rope, 2, dtype=F32) / rope)
    ang = positions.astype(F32).reshape(m, 1) * inv_freq
    zeros = jnp.zeros((m, LANES - rope), F32)
    cos_t = jnp.concatenate([jnp.cos(ang), jnp.cos(ang), zeros], axis=-1)
    sin_t = jnp.concatenate([-jnp.sin(ang), jnp.sin(ang), zeros], axis=-1)

    x = x.reshape(m, d)
    kv_mem = mem_kv(mem.reshape(batch * mem_len, d), norm_mem, w_mem_kv, g_mem_k)

    out_proj = functools.partial(matmul, out_dtype=F32, tm=OUT_PROJ_TM, tn=OUT_PROJ_TN, emit_stats=True)
    h, = rmsnorm_cast(x, norm_mix[0:1])
    u = _wide([h], a_w_in[0], out_dtype=F32, name="a_in_proj")
    conv_args = (u, a_conv_w[0], a_conv_b[0], a_ln_g[0], a_ln_b[0])
    conv_steps = m // _tile(seq, CONV_TS, CONV_HALO)
    if slab_rows(w_mlp_in.shape[1], conv_steps) is not None and slab_rows(a_w_out.shape[1], conv_steps) is not None:
        y_main, w_in0, w_a_out = conformer_conv(*conv_args, seq=seq, side_casts=((w_mlp_in, 0), (a_w_out, 0)))
    else:
        y_main, w_in0, w_a_out = conformer_conv(*conv_args, seq=seq), cast(w_mlp_in[0]), cast(a_w_out[0])
    y_mem = mem_attention(u, 2 * conv_ch // mem_w, kv_mem, 0, g_mem_q[0], seq=seq, mem_len=mem_len)
    x, xg, ssq = out_proj([y_main, y_mem], w_a_out, residual=x, next_gain=norm_mlp[0], name="a_out_proj")
    x, xb, ssq = _mlp(x, xg, ssq, w_in0, w_mlp_out, 0, emit_stats=True)

    c_kv, k_rope = kv_down(xb, ssq, scale_rows_cast(w_dkv_pad, kv_g_in), kv_g_a, pad_gain(kv_g_kr), cos_t, sin_t)
    k_all, v_all = kv_up(c_kv, w_kn, w_v, k_rope, kv_g_kn, n_heads=n_heads)
    u = matmul([xb], scale_rows_cast(b_w_in[0], norm_mix[1]), norm_ssq=ssq, out_dtype=F32, tm=B_IN_TM, tn=B_IN_TN,
               name="b_in_proj")
    q_kw = dict(n_heads=n_heads, scale=(nope + rope) ** -0.5 * math.log2(math.e))
    q_args = (u, b_g_qa[0], w_q_pad, b_g_qn[0], pad_gain(b_g_qr[0]), cos_t, sin_t)
    if slab_rows(b_w_out.shape[1], m // _tile(m, 512, SUBLANES)) is not None:
        q_all, w_b_out = q_proj(*q_args, side_cast=b_w_out[0], **q_kw)
    else:
        q_all, w_b_out = q_proj(*q_args, **q_kw), cast(b_w_out[0])
    if attention_cast_heads(w_mlp_in.shape[1], batch, seq, n_heads):
        y_main, w_in1 = mla_attention(q_all, k_all, v_all, batch=batch, seq=seq, n_heads=n_heads,
                                      side_cast=(w_mlp_in, 1))
    else:
        y_main = mla_attention(q_all, k_all, v_all, batch=batch, seq=seq, n_heads=n_heads)
        w_in1 = cast(w_mlp_in[1])
    y_mem = mem_attention(u, q_lora // mem_w, kv_mem, 1, g_mem_q[1], seq=seq, mem_len=mem_len)
    x, xg, ssq = out_proj([y_main, y_mem], w_b_out, residual=x, next_gain=norm_mlp[1], name="b_out_proj")
    x = _mlp(x, xg, ssq, w_in1, w_mlp_out, 1, emit_stats=False)
    return x.reshape(batch, seq, d)
```

```python
import functools
import math

import jax
import jax.numpy as jnp
from jax import lax
from jax.experimental import pallas as pl
from jax.experimental.pallas import tpu as pltpu

EPS = 1e-6
ROPE_THETA = 10000.0
LANES = 128
SUBLANES = 8
BF16_SUBLANES = 16
MIB = 1 << 20
VMEM_LIMIT_BYTES = 56 * MIB
KSPLIT_STATS_VMEM_BYTES = 62 * MIB
F32 = jnp.float32
BF16 = jnp.bfloat16


def _tile(n, pref, mult=LANES):
    if n <= pref:
        return n
    t = (pref // mult) * mult
    while t >= mult:
        if n % t == 0:
            return t
        t -= mult
    raise ValueError(f"no tile for {n} (pref {pref}, mult {mult})")


def _params(*semantics, vmem_limit_bytes=VMEM_LIMIT_BYTES):
    return pltpu.CompilerParams(dimension_semantics=semantics, vmem_limit_bytes=vmem_limit_bytes)


def _rms(x):
    ms = jnp.sum(x * x, axis=-1, keepdims=True) * (1.0 / x.shape[-1])
    return x * lax.rsqrt(ms + EPS)


def _rmsnorm_kernel(x_ref, g_ref, *o_refs):
    xn = _rms(x_ref[...])
    for i, o_ref in enumerate(o_refs):
        o_ref[...] = (xn * g_ref[i:i + 1, :]).astype(o_ref.dtype)


def rmsnorm_cast(x, gains):
    m, d = x.shape
    n = gains.shape[0]
    tm = _tile(m, 512, SUBLANES)
    return pl.pallas_call(
        _rmsnorm_kernel,
        grid=(m // tm,),
        in_specs=[pl.BlockSpec((tm, d), lambda i: (i, 0)),
                  pl.BlockSpec((n, d), lambda i: (0, 0))],
        out_specs=[pl.BlockSpec((tm, d), lambda i: (i, 0)) for _ in range(n)],
        out_shape=[jax.ShapeDtypeStruct((m, d), BF16) for _ in range(n)],
        compiler_params=_params("parallel"),
        name="rmsnorm_cast",
    )(x, gains)


def _scale_cast_kernel(w_ref, g_ref, o_ref):
    g = g_ref[...]
    for c in range(w_ref.shape[1] // LANES):
        cols = slice(c * LANES, (c + 1) * LANES)
        o_ref[:, cols] = (w_ref[:, cols] * g).astype(o_ref.dtype)


def scale_rows_cast(w, gain):
    k, n = w.shape
    tk = _tile(k, 512, SUBLANES)
    return pl.pallas_call(
        _scale_cast_kernel,
        grid=(k // tk,),
        in_specs=[pl.BlockSpec((tk, n), lambda i: (i, 0)), pl.BlockSpec((tk, LANES), lambda i: (i, 0))],
        out_specs=pl.BlockSpec((tk, n), lambda i: (i, 0)),
        out_shape=jax.ShapeDtypeStruct((k, n), BF16),
        compiler_params=_params("parallel"),
        name="scale_rows_cast",
    )(w, jnp.broadcast_to(gain.reshape(k, 1), (k, LANES)))


def _row_stats(x, xb_ref, ssq_ref, first_col_tile, gain=None):
    xb_ref[...] = (x if gain is None else x * gain).astype(xb_ref.dtype)
    sq = x * x
    partial = functools.reduce(lambda p, q: p + q,
                               [sq[:, c * LANES:(c + 1) * LANES] for c in range(x.shape[1] // LANES)])

    @pl.when(first_col_tile)
    def _():
        ssq_ref[...] = partial

    @pl.when(jnp.logical_not(first_col_tile))
    def _():
        ssq_ref[...] += partial


def _matmul_kernel(*refs, names, act, norm_dim):
    r = dict(zip(names, refs))
    if "side_in" in r:
        r["side_out"][...] = r["side_in"][...].astype(BF16)
    acc = None
    for key in names:
        if not key.startswith("a"):
            continue
        part = jnp.dot(r[key][...], r["w" + key[1:]][...].astype(BF16), preferred_element_type=F32)
        acc = part if acc is None else acc + part
    if "ssq" in r:
        acc = acc * lax.rsqrt(jnp.sum(r["ssq"][...], axis=-1, keepdims=True) * (1.0 / norm_dim) + EPS)
    if act == "relu2":
        acc = jnp.square(jnp.maximum(acc, 0.0))
    if "res" in r:
        acc = acc + r["res"][...]
    r["out"][...] = acc.astype(r["out"].dtype)
    if "xb" in r:
        _row_stats(acc, r["xb"], r["ssq_out"], pl.program_id(1) == 0,
                   gain=r["next_gain"][...] if "next_gain" in r else None)


def matmul(a_parts, w, *, out_dtype, act=None, residual=None, tm, tn, single_buffer_a=False,
           side_cast=None, norm_ssq=None, emit_stats=False, next_gain=None, name):
    m = a_parts[0].shape[0]
    n = w.shape[-1]
    tm, tn = _tile(m, tm, SUBLANES), _tile(n, tn)
    nj = n // tn
    a_mode = dict(pipeline_mode=pl.Buffered(1)) if single_buffer_a else {}
    names, in_specs, args = [], [], []

    def add(name_, spec, arr):
        names.append(name_)
        in_specs.append(spec)
        args.append(arr)

    offset = 0
    for p, a in enumerate(a_parts):
        kp = a.shape[1]
        assert offset % kp == 0
        add(f"a{p}", pl.BlockSpec((tm, kp), lambda i, j: (i, 0), **a_mode), a)
        add(f"w{p}", pl.BlockSpec((kp, tn), lambda i, j, blk=offset // kp: (blk, j)), w)
        offset += kp
    kdim = offset
    assert kdim == w.shape[-2]
    if norm_ssq is not None:
        assert len(a_parts) == 1
        add("ssq", pl.BlockSpec((tm, LANES), lambda i, j: (i, 0)), norm_ssq)
    if next_gain is not None:
        assert emit_stats
        add("next_gain", pl.BlockSpec((1, tn), lambda i, j: (0, j)), next_gain.reshape(1, n))
    if residual is not None:
        add("res", pl.BlockSpec((tm, tn), lambda i, j: (i, j)), residual)
    if side_cast is not None:
        src, src_layer = side_cast
        slab = src.shape[1] // ((m // tm) * nj)
        add("side_in", pl.BlockSpec((None, slab, src.shape[2]), lambda i, j: (src_layer, i * nj + j, 0)), src)
    names.append("out")
    out_specs = [pl.BlockSpec((tm, tn), lambda i, j: (i, j))]
    out_shape = [jax.ShapeDtypeStruct((m, n), out_dtype)]
    if side_cast is not None:
        names.append("side_out")
        out_specs.append(pl.BlockSpec((slab, src.shape[2]), lambda i, j: (i * nj + j, 0)))
        out_shape.append(jax.ShapeDtypeStruct(src.shape[1:], BF16))
    if emit_stats:
        assert out_dtype == F32
        names += ["xb", "ssq_out"]
        out_specs += [pl.BlockSpec((tm, tn), lambda i, j: (i, j)), pl.BlockSpec((tm, LANES), lambda i, j: (i, 0))]
        out_shape += [jax.ShapeDtypeStruct((m, n), BF16), jax.ShapeDtypeStruct((m, LANES), F32)]
    outs = pl.pallas_call(
        functools.partial(_matmul_kernel, names=tuple(names), act=act, norm_dim=kdim),
        grid=(m // tm, nj),
        in_specs=in_specs,
        out_specs=out_specs,
        out_shape=out_shape,
        compiler_params=_params("parallel", "arbitrary"),
        name=name,
    )(*args)
    return outs if len(outs) > 1 else outs[0]


def side_cast_fits(rows, m, n, tm, tn):
    return slab_rows(rows, (m // _tile(m, tm, SUBLANES)) * (n // _tile(n, tn))) is not None


def slab_rows(rows, steps):
    if steps <= 0 or rows % steps or (rows // steps) % BF16_SUBLANES:
        return None
    return rows // steps


def _matmul_ksplit_kernel(a_ref, w_ref, r_ref, o_ref, *stats_refs):
    k = pl.program_id(2)

    @pl.when(k == 0)
    def _():
        o_ref[...] = r_ref[...]

    o_ref[...] += jnp.dot(a_ref[...], w_ref[...], preferred_element_type=F32)
    if stats_refs:
        @pl.when(k == pl.num_programs(2) - 1)
        def _():
            _row_stats(o_ref[...], *stats_refs, pl.program_id(1) == 0)


def matmul_ksplit(a, w, residual, *, tm, tn, tk, emit_stats=False, name):
    m, kdim = a.shape
    n = w.shape[1]
    tm, tn, tk = _tile(m, tm, SUBLANES), _tile(n, tn), _tile(kdim, tk)
    nj = n // tn
    out_specs = [pl.BlockSpec((tm, tn), lambda i, j, k: (i, j))]
    out_shape = [jax.ShapeDtypeStruct((m, n), F32)]
    if emit_stats:
        out_specs += [pl.BlockSpec((tm, tn), lambda i, j, k: (i, j)),
                      pl.BlockSpec((tm, LANES), lambda i, j, k: (i, 0))]
        out_shape += [jax.ShapeDtypeStruct((m, n), BF16), jax.ShapeDtypeStruct((m, LANES), F32)]
    outs = pl.pallas_call(
        _matmul_ksplit_kernel,
        grid=(m // tm, nj, kdim // tk),
        in_specs=[pl.BlockSpec((tm, tk), lambda i, j, k: (i, k)),
                  pl.BlockSpec((tk, tn), lambda i, j, k: (k, j)),
                  pl.BlockSpec((tm, tn), lambda i, j, k: (i, j))],
        out_specs=out_specs,
        out_shape=out_shape,
        compiler_params=_params("parallel", "arbitrary", "arbitrary",
                                vmem_limit_bytes=KSPLIT_STATS_VMEM_BYTES if emit_stats else VMEM_LIMIT_BYTES),
        name=name,
    )(a, w, residual)
    return outs if emit_stats else outs[0]


CONV_ROWS = 64
CONV_HALO = 32
CONV_TS = 256
CONV_LANES = 256


def _sigmoid(x):
    return 0.5 * jnp.tanh(0.5 * x) + 0.5


def _conv_kernel(*refs, ts, seq, cw, lc, n_side):
    a_ref, gate_ref, w_ref, cb_ref, lng_ref, lnb_ref = refs[:6]
    side_ins = refs[6:6 + n_side]
    o_ref = refs[6 + n_side]
    side_outs = refs[7 + n_side:7 + 2 * n_side]
    buf_ref, c_ref = refs[7 + 2 * n_side:]
    for side_in, side_out in zip(side_ins, side_outs):
        side_out[...] = side_in[...].astype(BF16)
    i = pl.program_id(0)
    at_seq_start = (i * ts) % seq == 0

    @pl.when(at_seq_start)
    def _():
        buf_ref[0:CONV_HALO, :] = jnp.zeros((CONV_HALO, buf_ref.shape[1]), F32)

    @pl.when(jnp.logical_not(at_seq_start))
    def _():
        buf_ref[0:CONV_HALO, :] = buf_ref[ts:ts + CONV_HALO, :]

    buf_ref[CONV_HALO:CONV_HALO + ts, :] = a_ref[...] * _sigmoid(gate_ref[...])
    lead = CONV_HALO - (cw - 1)
    n_ch = a_ref.shape[1]

    def chunk(r, carry):
        r0 = pl.multiple_of(r * CONV_ROWS, CONV_ROWS)
        for c0 in range(0, n_ch, lc):
            cols = slice(c0, c0 + lc)
            acc = None
            for phase in range(SUBLANES):
                taps = [j for j in range(cw) if (lead + j) % SUBLANES == phase]
                if not taps:
                    continue
                rows = CONV_ROWS + (SUBLANES if phase else 0)
                part = None
                for j in taps:
                    base = (lead + j) // SUBLANES * SUBLANES
                    x = buf_ref[pl.ds(r0 + base, rows), cols].reshape(rows // SUBLANES, SUBLANES, lc)
                    term = x * w_ref[j, :, cols][None]
                    part = term if part is None else part + term
                part = part.reshape(rows, lc)
                piece = part[phase:phase + CONV_ROWS] if phase else part
                acc = piece if acc is None else acc + piece
            c_ref[pl.ds(r0, CONV_ROWS), cols] = acc + cb_ref[:, cols]
        c = c_ref[pl.ds(r0, CONV_ROWS), :]
        mu = jnp.mean(c, axis=-1, keepdims=True)
        cc = c - mu
        var = jnp.mean(cc * cc, axis=-1, keepdims=True)
        y = cc * lax.rsqrt(var + EPS) * lng_ref[...] + lnb_ref[...]
        o_ref[pl.ds(r0, CONV_ROWS), :] = (y * _sigmoid(y)).astype(o_ref.dtype)
        return carry

    lax.fori_loop(0, ts // CONV_ROWS, chunk, 0)


def conformer_conv(u, conv_w, conv_b, ln_g, ln_b, *, seq, side_casts=()):
    m = u.shape[0]
    cw, c = conv_w.shape
    assert cw - 1 <= CONV_HALO
    ts = _tile(seq, CONV_TS, CONV_HALO)
    lc = _tile(c, CONV_LANES)
    row = lambda v: v.reshape(1, c)
    in_specs = [pl.BlockSpec((ts, c), lambda i: (i, 0)),
                pl.BlockSpec((ts, c), lambda i: (i, 1)),
                pl.BlockSpec((cw, SUBLANES, c), lambda i: (0, 0, 0)),
                pl.BlockSpec((1, c), lambda i: (0, 0)),
                pl.BlockSpec((1, c), lambda i: (0, 0)),
                pl.BlockSpec((1, c), lambda i: (0, 0))]
    args = [u, u, jnp.broadcast_to(conv_w[:, None, :], (cw, SUBLANES, c)), row(conv_b), row(ln_g), row(ln_b)]
    out_specs = [pl.BlockSpec((ts, c), lambda i: (i, 0))]
    out_shape = [jax.ShapeDtypeStruct((m, c), BF16)]
    for src, src_layer in side_casts:
        slab = slab_rows(src.shape[1], m // ts)
        in_specs.append(pl.BlockSpec((None, slab, src.shape[2]), lambda i, l=src_layer: (l, i, 0)))
        args.append(src)
        out_specs.append(pl.BlockSpec((slab, src.shape[2]), lambda i: (i, 0)))
        out_shape.append(jax.ShapeDtypeStruct(src.shape[1:], BF16))
    outs = pl.pallas_call(
        functools.partial(_conv_kernel, ts=ts, seq=seq, cw=cw, lc=lc, n_side=len(side_casts)),
        grid=(m // ts,),
        in_specs=in_specs,
        out_specs=out_specs,
        out_shape=out_shape,
        scratch_shapes=[pltpu.VMEM((CONV_HALO + ts, c), F32), pltpu.VMEM((ts, c), F32)],
        compiler_params=_params("arbitrary"),
        name="conformer_conv",
    )(*args)
    return outs if side_casts else outs[0]


def _mem_kv_kernel(mem_ref, gn_ref, w_ref, gk_ref, o_ref, *, n_k_heads):
    j = pl.program_id(1)
    hn = (_rms(mem_ref[...]) * gn_ref[0]).astype(BF16)
    kv = jnp.dot(hn, w_ref[0].astype(BF16), preferred_element_type=F32)
    kn = _rms(kv) * gk_ref[0]
    o_ref[0] = jnp.where(j < n_k_heads, kn, kv).astype(o_ref.dtype)


def mem_kv(mem2d, norm_mem, w_mem_kv, g_mem_k):
    nl, d, n2 = w_mem_kv.shape
    hd = g_mem_k.shape[-1]
    bm = mem2d.shape[0]
    return pl.pallas_call(
        functools.partial(_mem_kv_kernel, n_k_heads=n2 // 2 // hd),
        grid=(nl, n2 // hd),
        in_specs=[pl.BlockSpec((bm, d), lambda l, j: (0, 0)),
                  pl.BlockSpec((1, 1, d), lambda l, j: (l, 0, 0)),
                  pl.BlockSpec((1, d, hd), lambda l, j: (l, 0, j)),
                  pl.BlockSpec((1, 1, hd), lambda l, j: (l, 0, 0))],
        out_specs=pl.BlockSpec((1, bm, hd), lambda l, j: (l, 0, j)),
        out_shape=jax.ShapeDtypeStruct((nl, bm, n2), BF16),
        compiler_params=_params("parallel", "parallel"),
        name="mem_kv",
    )(mem2d, norm_mem.reshape(nl, 1, d), w_mem_kv, g_mem_k.reshape(nl, 1, hd))


def _mem_attn_kernel(q_ref, k_ref, v_ref, gq_ref, o_ref, *, n_heads, hd):
    scale = hd ** -0.5
    for h in range(n_heads):
        cols = slice(h * hd, (h + 1) * hd)
        q = (_rms(q_ref[:, cols]) * (gq_ref[...] * scale)).astype(BF16)
        s = lax.dot_general(q, k_ref[0, :, cols], (((1,), (1,)), ((), ())), preferred_element_type=F32)
        p = jnp.exp(s - jnp.max(s, axis=-1, keepdims=True))
        denom = jnp.sum(p, axis=-1, keepdims=True)
        o = jnp.dot(p.astype(BF16), v_ref[0, :, cols], preferred_element_type=F32)
        o_ref[:, cols] = (o / denom).astype(o_ref.dtype)


def mem_attention(qsrc, q_col_block, kv, layer, g_q, *, seq, mem_len):
    m = qsrc.shape[0]
    w = kv.shape[-1] // 2
    hd = g_q.shape[-1]
    ts = _tile(seq, 512, SUBLANES)
    tiles_per_seq = seq // ts
    return pl.pallas_call(
        functools.partial(_mem_attn_kernel, n_heads=w // hd, hd=hd),
        grid=(m // ts,),
        in_specs=[pl.BlockSpec((ts, w), lambda i: (i, q_col_block)),
                  pl.BlockSpec((1, mem_len, w), lambda i: (layer, i // tiles_per_seq, 0)),
                  pl.BlockSpec((1, mem_len, w), lambda i: (layer, i // tiles_per_seq, 1)),
                  pl.BlockSpec((1, hd), lambda i: (0, 0))],
        out_specs=pl.BlockSpec((ts, w), lambda i: (i, 0)),
        out_shape=jax.ShapeDtypeStruct((m, w), BF16),
        compiler_params=_params("parallel"),
        name="mem_attention",
    )(qsrc, kv, kv, g_q.reshape(1, hd))


def _rope(t, cos_ref, sin_ref):
    partner = pltpu.roll(t, LANES // 4, 1)
    return t * cos_ref[...] + partner * sin_ref[...]


def _dkv_kernel(xb_ref, ssq_ref, w_ref, ga_ref, gr_ref, cos_ref, sin_ref, ckv_ref, kr_ref, *, lora, dim):
    ckr = jnp.dot(xb_ref[...], w_ref[...], preferred_element_type=F32)
    ckr = ckr * lax.rsqrt(jnp.sum(ssq_ref[...], axis=-1, keepdims=True) * (1.0 / dim) + EPS)
    ckv_ref[...] = (_rms(ckr[:, :lora]) * ga_ref[...]).astype(ckv_ref.dtype)
    kr = _rms(ckr[:, lora:]) * gr_ref[...]
    kr_ref[...] = _rope(kr, cos_ref, sin_ref).astype(kr_ref.dtype)


def kv_down(xb, ssq, w_dkv_pad, g_a, g_kr_pad, cos_t, sin_t):
    m, d = xb.shape
    lora = g_a.shape[-1]
    tm = _tile(m, 1024, SUBLANES)
    full = lambda shape: pl.BlockSpec(shape, lambda i: (0, 0))
    rows = lambda width: pl.BlockSpec((tm, width), lambda i: (i, 0))
    return pl.pallas_call(
        functools.partial(_dkv_kernel, lora=lora, dim=d),
        grid=(m // tm,),
        in_specs=[rows(d), rows(LANES), full(w_dkv_pad.shape), full((1, lora)), full((1, LANES)),
                  rows(LANES), rows(LANES)],
        out_specs=[rows(lora), rows(LANES)],
        out_shape=[jax.ShapeDtypeStruct((m, lora), BF16), jax.ShapeDtypeStruct((m, LANES), BF16)],
        compiler_params=_params("parallel"),
        name="kv_down",
    )(xb, ssq, w_dkv_pad, g_a.reshape(1, lora), g_kr_pad, cos_t, sin_t)


HEAD_PAIR = 2


def _ukv_kernel(c_ref, wk_ref, wv_ref, kr_ref, gk_ref, k_ref, v_ref, *, heads, nope):
    c = c_ref[...]
    kr = kr_ref[...]
    width = HEAD_PAIR * nope
    for pair in range(heads // HEAD_PAIR):
        kn = jnp.dot(c, wk_ref[:, pair * width:(pair + 1) * width], preferred_element_type=F32)
        for h in range(HEAD_PAIR):
            base = (pair * HEAD_PAIR + h) * 2 * nope
            k_ref[:, base:base + nope] = (_rms(kn[:, h * nope:(h + 1) * nope]) * gk_ref[...]).astype(k_ref.dtype)
            k_ref[:, base + nope:base + 2 * nope] = kr
        cols = slice(pair * width, (pair + 1) * width)
        v_ref[:, cols] = jnp.dot(c, wv_ref[:, cols], preferred_element_type=F32).astype(v_ref.dtype)


def kv_up(c_kv, w_kn, w_v, kr, g_kn, *, n_heads):
    m, lora = c_kv.shape
    nope = g_kn.shape[-1]
    dv = w_v.shape[1] // n_heads
    assert nope == LANES and dv == LANES and n_heads % HEAD_PAIR == 0
    tm = _tile(m, 512, SUBLANES)
    whole = lambda shape: pl.BlockSpec(shape, lambda i: (0, 0))
    rows = lambda width: pl.BlockSpec((tm, width), lambda i: (i, 0))
    return pl.pallas_call(
        functools.partial(_ukv_kernel, heads=n_heads, nope=nope),
        grid=(m // tm,),
        in_specs=[rows(lora), whole(w_kn.shape), whole(w_v.shape), rows(LANES), whole((1, nope))],
        out_specs=[rows(n_heads * 2 * nope), rows(n_heads * dv)],
        out_shape=[jax.ShapeDtypeStruct((m, n_heads * 2 * nope), BF16),
                   jax.ShapeDtypeStruct((m, n_heads * dv), BF16)],
        compiler_params=_params("parallel"),
        name="kv_up",
    )(c_kv, w_kn, w_v, kr, g_kn.reshape(1, nope))


def _q_kernel(*refs, heads, nope, scale, has_side):
    if has_side:
        u_ref, ga_ref, w_ref, gn_ref, gr_ref, cos_ref, sin_ref, side_in, q_ref, side_out, cq_ref = refs
        side_out[...] = side_in[...].astype(BF16)
    else:
        u_ref, ga_ref, w_ref, gn_ref, gr_ref, cos_ref, sin_ref, q_ref, cq_ref = refs
    cq_ref[...] = (_rms(u_ref[...]) * ga_ref[...]).astype(cq_ref.dtype)
    width = HEAD_PAIR * 2 * nope
    for pair in range(heads // HEAD_PAIR):
        q = jnp.dot(cq_ref[...], w_ref[:, pair * width:(pair + 1) * width], preferred_element_type=F32)
        for h in range(HEAD_PAIR):
            base = h * 2 * nope
            out = pair * width + base
            qn = _rms(q[:, base:base + nope]) * (gn_ref[...] * scale)
            q_ref[:, out:out + nope] = qn.astype(q_ref.dtype)
            qr = _rms(q[:, base + nope:base + 2 * nope]) * (gr_ref[...] * scale)
            q_ref[:, out + nope:out + 2 * nope] = _rope(qr, cos_ref, sin_ref).astype(q_ref.dtype)


def q_proj(u, g_qa, w_q_pad, g_qn, g_qr_pad, cos_t, sin_t, *, n_heads, scale, side_cast=None):
    m = u.shape[0]
    lora = g_qa.shape[-1]
    nope = g_qn.shape[-1]
    assert nope == LANES and n_heads % HEAD_PAIR == 0
    tm = _tile(m, 512, SUBLANES)
    whole = lambda shape: pl.BlockSpec(shape, lambda i: (0, 0))
    rows_of = lambda height, width: pl.BlockSpec((height, width), lambda i: (i, 0))
    rows = lambda width: rows_of(tm, width)
    in_specs = [rows(lora), whole((1, lora)), whole(w_q_pad.shape), whole((1, nope)), whole((1, LANES)),
                rows(LANES), rows(LANES)]
    args = [u, g_qa.reshape(1, lora), w_q_pad, g_qn.reshape(1, nope), g_qr_pad, cos_t, sin_t]
    out_specs = [rows(n_heads * 2 * nope)]
    out_shape = [jax.ShapeDtypeStruct((m, n_heads * 2 * nope), BF16)]
    if side_cast is not None:
        slab = slab_rows(side_cast.shape[0], m // tm)
        in_specs.append(rows_of(slab, side_cast.shape[1]))
        args.append(side_cast)
        out_specs.append(rows_of(slab, side_cast.shape[1]))
        out_shape.append(jax.ShapeDtypeStruct(side_cast.shape, BF16))
    outs = pl.pallas_call(
        functools.partial(_q_kernel, heads=n_heads, nope=nope, scale=scale, has_side=side_cast is not None),
        grid=(m // tm,),
        in_specs=in_specs,
        out_specs=out_specs,
        out_shape=out_shape,
        scratch_shapes=[pltpu.VMEM((tm, lora), BF16)],
        compiler_params=_params("parallel"),
        name="q_proj",
    )(*args)
    return outs if side_cast is not None else outs[0]


ATT_TQ = 4096
ATT_TK = 256
ATT_GROUP = 8


def _mla_kernel(*refs, tq, tk, group, has_side):
    if has_side:
        q_ref, k_ref, v_ref, side_in, o_ref, side_out, m_ref, l_ref, acc_ref = refs
        side_out[...] = side_in[...].astype(BF16)
    else:
        q_ref, k_ref, v_ref, o_ref, m_ref, l_ref, acc_ref = refs
    i = pl.program_id(2)
    m_ref[...] = jnp.full(m_ref.shape, -jnp.inf, F32)
    l_ref[...] = jnp.zeros(l_ref.shape, F32)
    acc_ref[...] = jnp.zeros(acc_ref.shape, F32)
    chunks = tk // LANES

    def sub_block(start, diag_offset=None):
        rows = slice(0 if diag_offset is None else diag_offset, tq)
        s = lax.dot_general(q_ref[rows, :], k_ref[pl.ds(start, tk), :], (((1,), (1,)), ((), ())),
                            preferred_element_type=F32)
        if diag_offset is not None:
            row = lax.broadcasted_iota(jnp.int32, (tk, tk), 0)
            col = lax.broadcasted_iota(jnp.int32, (tk, tk), 1)
            top = jnp.where(col <= row, s[:tk], -jnp.inf)
            s = top if s.shape[0] == tk else jnp.concatenate([top, s[tk:]], axis=0)
        m_prev = m_ref[rows, :]
        m_new = jnp.maximum(m_prev, jnp.max(s, axis=-1, keepdims=True))
        alpha = jnp.exp2(m_prev - m_new)
        p = [jnp.exp2(s[:, c * LANES:(c + 1) * LANES] - m_new) for c in range(chunks)]
        l_ref[rows, :] = alpha * l_ref[rows, :] + functools.reduce(lambda a, b: a + b, p)
        pv = jnp.dot(jnp.concatenate(p, axis=1).astype(BF16), v_ref[pl.ds(start, tk), :],
                     preferred_element_type=F32)
        acc_ref[rows, :] = alpha * acc_ref[rows, :] + pv
        m_ref[rows, :] = m_new

    per_tile = tq // tk
    n_off = i * per_tile
    n_groups = n_off // group

    def body(g, carry):
        for t in range(group):
            sub_block(pl.multiple_of((g * group + t) * tk, tk))
        return carry

    lax.fori_loop(0, n_groups, body, 0)
    step = per_tile
    while group % step:
        step -= 1
    for rem in range(step, group, step):
        @pl.when(n_off % group == rem)
        def _(rem=rem):
            for t in range(rem):
                sub_block(pl.multiple_of((n_groups * group + t) * tk, tk))
    for t in range(per_tile):
        sub_block(pl.multiple_of(i * tq + t * tk, tk), diag_offset=t * tk)
    denom = jnp.sum(l_ref[...], axis=-1, keepdims=True)
    o_ref[...] = (acc_ref[...] / denom).astype(o_ref.dtype)


def attention_cast_heads(rows, batch, seq, n_heads):
    nq = seq // _tile(seq, ATT_TQ, ATT_TK)
    for heads in range(n_heads, 0, -1):
        if slab_rows(rows, batch * heads * nq) is not None:
            return heads
    return 0


def mla_attention(q, k, v, *, batch, seq, n_heads, side_cast=None):
    m = q.shape[0]
    dqk = q.shape[1] // n_heads
    dv = v.shape[1] // n_heads
    assert dv == LANES
    tq = _tile(seq, ATT_TQ, ATT_TK)
    tk = min(ATT_TK, tq)
    nq = seq // tq
    in_specs = [pl.BlockSpec((tq, dqk), lambda b, h, i: (b * nq + i, h)),
                pl.BlockSpec((seq, dqk), lambda b, h, i: (b, h)),
                pl.BlockSpec((seq, dv), lambda b, h, i: (b, h))]
    args = [q, k, v]
    out_specs = [pl.BlockSpec((tq, dv), lambda b, h, i: (b * nq + i, h))]
    out_shape = [jax.ShapeDtypeStruct((m, n_heads * dv), BF16)]
    if side_cast is not None:
        src, src_layer = side_cast
        heads = attention_cast_heads(src.shape[1], batch, seq, n_heads)
        slab = slab_rows(src.shape[1], batch * heads * nq)

        def slab_index(b, h, i):
            done = h >= heads
            return (b * heads + jnp.minimum(h, heads - 1)) * nq + jnp.where(done, nq - 1, i)

        in_specs.append(pl.BlockSpec((None, slab, src.shape[2]), lambda b, h, i: (src_layer, slab_index(b, h, i), 0)))
        args.append(src)
        out_specs.append(pl.BlockSpec((slab, src.shape[2]), lambda b, h, i: (slab_index(b, h, i), 0)))
        out_shape.append(jax.ShapeDtypeStruct(src.shape[1:], BF16))
    outs = pl.pallas_call(
        functools.partial(_mla_kernel, tq=tq, tk=tk, group=ATT_GROUP, has_side=side_cast is not None),
        grid=(batch, n_heads, nq),
        in_specs=in_specs,
        out_specs=out_specs,
        out_shape=out_shape,
        scratch_shapes=[pltpu.VMEM((tq, LANES), F32), pltpu.VMEM((tq, LANES), F32), pltpu.VMEM((tq, dv), F32)],
        compiler_params=_params("parallel", "arbitrary", "arbitrary"),
        name="mla_attention",
    )(*args)
    return outs if side_cast is not None else outs[0]


WIDE_TM, WIDE_TN = 2048, 512
OUT_PROJ_TM, OUT_PROJ_TN = 1024, 512
B_IN_TM, B_IN_TN = 1024, 1024
MLP_IN_TM, MLP_IN_TN = 1024, 1024
MLP_OUT_TM, MLP_OUT_TN, MLP_OUT_TK = 1024, 1024, 4096


def _wide(a_parts, w_f32, **kw):
    return matmul(a_parts, w_f32, tm=WIDE_TM, tn=WIDE_TN, single_buffer_a=True, **kw)


def _mlp(x, xg, ssq, w_in_bf16, w_out, layer, emit_stats):
    kw = dict(norm_ssq=ssq, out_dtype=BF16, act="relu2", tm=MLP_IN_TM, tn=MLP_IN_TN, name="mlp_in")
    if side_cast_fits(w_out.shape[1], x.shape[0], w_in_bf16.shape[-1], MLP_IN_TM, MLP_IN_TN):
        hm, w_out_bf16 = matmul([xg], w_in_bf16, side_cast=(w_out, layer), **kw)
    else:
        hm, w_out_bf16 = matmul([xg], w_in_bf16, **kw), w_out[layer].astype(BF16)
    return matmul_ksplit(hm, w_out_bf16, x, tm=MLP_OUT_TM, tn=MLP_OUT_TN, tk=MLP_OUT_TK, emit_stats=emit_stats,
                         name="mlp_out")


def kernel(x, mem, positions, norm_mix, norm_mlp, norm_mem, w_mem_kv, g_mem_q, g_mem_k, w_mlp_in, w_mlp_out,
           a_w_in, a_conv_w, a_conv_b, a_ln_g, a_ln_b, a_w_out, b_w_in, b_g_qa, b_w_uq, b_g_qn, b_g_qr, b_w_out,
           kv_g_in, kv_w_dkv, kv_g_a, kv_w_ukv, kv_g_kn, kv_g_kr):
    batch, seq, d = x.shape
    m = batch * seq
    mem_len = mem.shape[1]
    mem_w = w_mem_kv.shape[-1] // 2
    conv_ch = a_conv_w.shape[-1]
    q_lora = b_g_qa.shape[-1]
    kv_lora = kv_g_a.shape[-1]
    nope, rope = b_g_qn.shape[-1], b_g_qr.shape[-1]
    n_heads = b_w_uq.shape[-1] // (nope + rope)
    dv = kv_w_ukv.shape[-1] // n_heads - nope
    assert 2 * rope == LANES and nope == LANES and dv == LANES
    assert (2 * conv_ch) % mem_w == 0 and q_lora % mem_w == 0
    assert b_w_in.shape[0] == 1 and a_w_in.shape[0] == 1 and norm_mix.shape[0] == 2

    cast = lambda w: w.astype(BF16)
    w_uq = b_w_uq[0].reshape(q_lora, n_heads, nope + rope)
    w_q_pad = cast(jnp.concatenate([w_uq, w_uq[:, :, nope:]], axis=-1).reshape(q_lora, n_heads * 2 * nope))
    w_ukv = kv_w_ukv.reshape(kv_lora, n_heads, nope + dv)
    w_kn = cast(w_ukv[:, :, :nope].reshape(kv_lora, n_heads * nope))
    w_v = cast(w_ukv[:, :, nope:].reshape(kv_lora, n_heads * dv))
    w_dkv_pad = jnp.concatenate([kv_w_dkv, kv_w_dkv[:, kv_lora:]], axis=-1)
    pad_gain = lambda g: jnp.concatenate([g, g]).reshape(1, 2 * rope)

    inv_freq = ROPE_THETA ** (-jnp.arange(0, rope, 2, dtype=F32) / rope)
    ang = positions.astype(F32).reshape(m, 1) * inv_freq
    zeros = jnp.zeros((m, LANES - rope), F32)
    cos_t = jnp.concatenate([jnp.cos(ang), jnp.cos(ang), zeros], axis=-1)
    sin_t = jnp.concatenate([-jnp.sin(ang), jnp.sin(ang), zeros], axis=-1)

    x = x.reshape(m, d)
    kv_mem = mem_kv(mem.reshape(batch * mem_len, d), norm_mem, w_mem_kv, g_mem_k)

    out_proj = functools.partial(matmul, out_dtype=F32, tm=OUT_PROJ_TM, tn=OUT_PROJ_TN, emit_stats=True)
    h, = rmsnorm_cast(x, norm_mix[0:1])
    u = _wide([h], a_w_in[0], out_dtype=F32, name="a_in_proj")
    conv_args = (u, a_conv_w[0], a_conv_b[0], a_ln_g[0], a_ln_b[0])
    conv_steps = m // _tile(seq, CONV_TS, CONV_HALO)
    if slab_rows(w_mlp_in.shape[1], conv_steps) is not None and slab_rows(a_w_out.shape[1], conv_steps) is not None:
        y_main, w_in0, w_a_out = conformer_conv(*conv_args, seq=seq, side_casts=((w_mlp_in, 0), (a_w_out, 0)))
    else:
        y_main, w_in0, w_a_out = conformer_conv(*conv_args, seq=seq), cast(w_mlp_in[0]), cast(a_w_out[0])
    y_mem = mem_attention(u, 2 * conv_ch // mem_w, kv_mem, 0, g_mem_q[0], seq=seq, mem_len=mem_len)
    x, xg, ssq = out_proj([y_main, y_mem], w_a_out, residual=x, next_gain=norm_mlp[0], name="a_out_proj")
    x, xb, ssq = _mlp(x, xg, ssq, w_in0, w_mlp_out, 0, emit_stats=True)

    c_kv, k_rope = kv_down(xb, ssq, scale_rows_cast(w_dkv_pad, kv_g_in), kv_g_a, pad_gain(kv_g_kr), cos_t, sin_t)
    k_all, v_all = kv_up(c_kv, w_kn, w_v, k_rope, kv_g_kn, n_heads=n_heads)
    u = matmul([xb], scale_rows_cast(b_w_in[0], norm_mix[1]), norm_ssq=ssq, out_dtype=F32, tm=B_IN_TM, tn=B_IN_TN,
               name="b_in_proj")
    q_kw = dict(n_heads=n_heads, scale=(nope + rope) ** -0.5 * math.log2(math.e))
    q_args = (u, b_g_qa[0], w_q_pad, b_g_qn[0], pad_gain(b_g_qr[0]), cos_t, sin_t)
    if slab_rows(b_w_out.shape[1], m // _tile(m, 512, SUBLANES)) is not None:
        q_all, w_b_out = q_proj(*q_args, side_cast=b_w_out[0], **q_kw)
    else:
        q_all, w_b_out = q_proj(*q_args, **q_kw), cast(b_w_out[0])
    if attention_cast_heads(w_mlp_in.shape[1], batch, seq, n_heads):
        y_main, w_in1 = mla_attention(q_all, k_all, v_all, batch=batch, seq=seq, n_heads=n_heads,
                                      side_cast=(w_mlp_in, 1))
    else:
        y_main = mla_attention(q_all, k_all, v_all, batch=batch, seq=seq, n_heads=n_heads)
        w_in1 = cast(w_mlp_in[1])
    y_mem = mem_attention(u, q_lora // mem_w, kv_mem, 1, g_mem_q[1], seq=seq, mem_len=mem_len)
    x, xg, ssq = out_proj([y_main, y_mem], w_b_out, residual=x, next_gain=norm_mlp[1], name="b_out_proj")
    x = _mlp(x, xg, ssq, w_in1, w_mlp_out, 1, emit_stats=False)
    return x.reshape(batch, seq, d)
```
